```python
import math
import jax
import jax.numpy as jnp
from jax import lax
import numpy as np

D_MODEL = 1024
BATCH = 2
SEQ = 8192
DEPTH = 2

CHUNK = 64
Q_BLOCK = 128
HEAD_DIM = 64
N_HEADS_SB = 8
N_HEADS_DIFF = 4
N_HEADS_CHUNK = D_MODEL // HEAD_DIM
D_SB = N_HEADS_SB * HEAD_DIM
D_DIFF = N_HEADS_DIFF * 2 * HEAD_DIM
D_MIX = D_SB + D_DIFF
D_IN_EVEN = 3 * D_SB + 3 * D_DIFF
ROPE_THETA = 500000.0
ROT_DIM = HEAD_DIM // 4
LEFT_CHUNKS = 8
BAND = (LEFT_CHUNKS + 1) * CHUNK
REL_CLIP = 128
N_REL = 2 * REL_CLIP + 1
D_FF_DENSE = 2816
N_EXPERTS = 8
TOP_K = 2
D_FF_EXPERT = 3584
RMS_EPS = 1e-6
N_EVEN = (DEPTH + 1) // 2
N_ODD = DEPTH // 2

kernel_name = 'hybrid_stickbreak_diff_chunkband_moe_trunk'

F32 = jnp.float32


def rms_norm(x, g):
    xf = x.astype(F32)
    y = xf * lax.rsqrt(jnp.mean(xf * xf, axis=-1, keepdims=True) + RMS_EPS)
    return (y * g.astype(F32)).astype(x.dtype)


def rope_tables(positions):
    inv_freq = ROPE_THETA ** (-jnp.arange(0, ROT_DIM, 2, dtype=F32) / ROT_DIM)
    ang = positions.astype(F32)[..., None] * inv_freq
    return jnp.cos(ang)[:, None], jnp.sin(ang)[:, None]


def partial_rope(x, cos, sin):
    half = ROT_DIM // 2
    x1 = x[..., :half].astype(F32)
    x2 = x[..., half:ROT_DIM].astype(F32)
    rot = jnp.concatenate([x1 * cos - x2 * sin, x2 * cos + x1 * sin], axis=-1)
    return jnp.concatenate([rot.astype(x.dtype), x[..., ROT_DIM:]], axis=-1)


def split_heads(t, n):
    b, s, _ = t.shape
    return t.reshape(b, s, n, -1).transpose(0, 2, 1, 3)


def merge_heads(o):
    b, h, s, d = o.shape
    return o.transpose(0, 2, 1, 3).reshape(b, s, h * d)


def to_query_blocks(q):
    b, h, s, d = q.shape
    return jnp.moveaxis(q.reshape(b, h, s // Q_BLOCK, Q_BLOCK, d), 2, 0)


def from_query_blocks(o):
    nb, b, h, qb, d = o.shape
    return jnp.moveaxis(o, 0, 2).reshape(b, h, nb * qb, d)


def stick_breaking_attention(q, k, v):
    s_len = q.shape[2]
    nb = s_len // Q_BLOCK
    scale = HEAD_DIM ** -0.5
    k_idx = jnp.arange(s_len)

    def block(args):
        q_blk, start = args
        z = jnp.einsum('bhqd,bhkd->bhqk', q_blk, k, preferred_element_type=F32) * scale
        t_idx = start + jnp.arange(Q_BLOCK)
        past = k_idx[None, :] < t_idx[:, None]
        log_stay = jnp.where(past, jax.nn.log_sigmoid(-z), 0.0)
        between = lax.cumsum(log_stay, axis=3, reverse=True) - log_stay
        w = jnp.where(past, jnp.exp(jax.nn.log_sigmoid(z) + between), 0.0)
        return jnp.einsum('bhqk,bhkd->bhqd', w.astype(v.dtype), v)

    out = lax.map(block, (to_query_blocks(q), jnp.arange(nb, dtype=jnp.int32) * Q_BLOCK))
    return from_query_blocks(out)


def differential_attention(q, k, v, lam):
    b, h, _, s_len, d = q.shape
    nb = s_len // Q_BLOCK
    scale = HEAD_DIM ** -0.5
    k_chunk = jnp.arange(s_len) // CHUNK
    q_blocks = jnp.moveaxis(q.reshape(b, h, 2, nb, Q_BLOCK, d), 3, 0)

    def block(args):
        q_blk, start = args
        sc = jnp.einsum('bhmqd,bhmkd->bhmqk', q_blk, k, preferred_element_type=F32) * scale
        t_chunk = (start + jnp.arange(Q_BLOCK)) // CHUNK
        visible = k_chunk[None, :] <= t_chunk[:, None]
        p = jax.nn.softmax(jnp.where(visible, sc, -jnp.inf), axis=-1)
        w = p[:, :, 0] - lam * p[:, :, 1]
        return jnp.einsum('bhqk,bhkv->bhqv', w.astype(v.dtype), v)

    out = lax.map(block, (q_blocks, jnp.arange(nb, dtype=jnp.int32) * Q_BLOCK))
    return from_query_blocks(out)


def chunk_band_attention(q, k, v, rel_table):
    b, h, s_len, d = q.shape
    nc = s_len // CHUNK
    pad = LEFT_CHUNKS * CHUNK
    scale = HEAD_DIM ** -0.5
    k_pad = jnp.pad(k, ((0, 0), (0, 0), (pad, 0), (0, 0)))
    v_pad = jnp.pad(v, ((0, 0), (0, 0), (pad, 0), (0, 0)))
    i = jnp.arange(CHUNK)
    j = jnp.arange(BAND)
    rel = jnp.clip(pad + i[:, None] - j[None, :], -REL_CLIP, REL_CLIP) + REL_CLIP
    bias = rel_table[:, rel].astype(F32)
    q_chunks = jnp.moveaxis(q.reshape(b, h, nc, CHUNK, d), 2, 0)

    def chunk(args):
        q_c, c = args
        start = c * CHUNK
        k_band = lax.dynamic_slice_in_dim(k_pad, start, BAND, axis=2)
        v_band = lax.dynamic_slice_in_dim(v_pad, start, BAND, axis=2)
        sc = jnp.einsum('bhqd,bhkd->bhqk', q_c, k_band, preferred_element_type=F32) * scale + bias
        valid = (start - pad + j) >= 0
        p = jax.nn.softmax(jnp.where(valid[None, None, None, :], sc, -jnp.inf), axis=-1)
        return jnp.einsum('bhqk,bhkd->bhqd', p.astype(v.dtype), v_band)

    out = lax.map(chunk, (q_chunks, jnp.arange(nc, dtype=jnp.int32)))
    return jnp.moveaxis(out, 0, 2).reshape(b, h, s_len, d)


def even_mixer(h, cos, sin, w_in, q_norm, k_norm, lam_q1, lam_k1, lam_q2, lam_k2, subln, w_out, layer):
    b, s, _ = h.shape
    proj = h @ w_in
    cuts = [D_SB, 2 * D_SB, 3 * D_SB, 3 * D_SB + D_DIFF, 3 * D_SB + 2 * D_DIFF]
    q_sb, k_sb, v_sb, q_df, k_df, v_df = jnp.split(proj, cuts, axis=-1)
    o_sb = stick_breaking_attention(split_heads(q_sb, N_HEADS_SB), split_heads(k_sb, N_HEADS_SB),
                                    split_heads(v_sb, N_HEADS_SB))
    qd = partial_rope(rms_norm(split_heads(q_df, 2 * N_HEADS_DIFF), q_norm), cos, sin)
    kd = partial_rope(rms_norm(split_heads(k_df, 2 * N_HEADS_DIFF), k_norm), cos, sin)
    qd = qd.reshape(b, N_HEADS_DIFF, 2, s, HEAD_DIM)
    kd = kd.reshape(b, N_HEADS_DIFF, 2, s, HEAD_DIM)
    vd = split_heads(v_df, N_HEADS_DIFF)
    lam_init = 0.8 - 0.6 * math.exp(-0.3 * layer)
    lam = (jnp.exp(jnp.sum(lam_q1.astype(F32) * lam_k1.astype(F32)))
           - jnp.exp(jnp.sum(lam_q2.astype(F32) * lam_k2.astype(F32))) + lam_init)
    o_df = rms_norm(differential_attention(qd, kd, vd, lam), subln) * (1.0 - lam_init)
    merged = jnp.concatenate([merge_heads(o_sb), merge_heads(o_df)], axis=-1)
    return merged @ w_out


def odd_mixer(h, w_qkv, q_norm, k_norm, rel_table, w_out):
    q, k, v = jnp.split(h @ w_qkv, 3, axis=-1)
    q = rms_norm(split_heads(q, N_HEADS_CHUNK), q_norm)
    k = rms_norm(split_heads(k, N_HEADS_CHUNK), k_norm)
    o = chunk_band_attention(q, k, split_heads(v, N_HEADS_CHUNK), rel_table)
    return merge_heads(o) @ w_out


def swiglu(h, w_gate, w_up, w_down):
    return (jax.nn.silu(h @ w_gate) * (h @ w_up)) @ w_down


def moe_swiglu(h, w_router, we_gate, we_up, we_down):
    b, s, d = h.shape
    t = h.reshape(b * s, d)
    logits = (t @ w_router).astype(F32)
    top_val, top_idx = lax.top_k(logits, TOP_K)
    gates = jax.nn.softmax(top_val, axis=-1)
    combine = jnp.einsum('tk,tke->te', gates, jax.nn.one_hot(top_idx, N_EXPERTS, dtype=F32))
    y = jnp.zeros_like(t)
    for e in range(N_EXPERTS):
        y = y + combine[:, e:e + 1].astype(t.dtype) * swiglu(t, we_gate[e], we_up[e], we_down[e])
    return y.reshape(b, s, d)


def setup_inputs(seed: int = 0) -> dict:
    key = jax.random.key(seed)
    ks = jax.random.split(key, 32)

    def nrm(k, shape, scale):
        return jax.random.normal(k, shape, F32) * scale

    def gain(k, shape):
        return 1.0 + 0.02 * jax.random.normal(k, shape, F32)

    x = nrm(ks[0], (BATCH, SEQ, D_MODEL), 1.0)
    offsets = jax.random.randint(ks[1], (BATCH, 1), 0, 16, dtype=jnp.int32) * CHUNK
    positions = offsets + jnp.arange(SEQ, dtype=jnp.int32)[None, :]
    return {
        'x': x,
        'positions': positions,
        'ev_attn_norm': gain(ks[2], (N_EVEN, D_MODEL)),
        'ev_w_in': nrm(ks[3], (N_EVEN, D_MODEL, D_IN_EVEN), D_MODEL ** -0.5),
        'ev_q_norm': gain(ks[4], (N_EVEN, HEAD_DIM)),
        'ev_k_norm': gain(ks[5], (N_EVEN, HEAD_DIM)),
        'ev_lambda_q1': nrm(ks[6], (N_EVEN, HEAD_DIM), 0.1),
        'ev_lambda_k1': nrm(ks[7], (N_EVEN, HEAD_DIM), 0.1),
        'ev_lambda_q2': nrm(ks[8], (N_EVEN, HEAD_DIM), 0.1),
        'ev_lambda_k2': nrm(ks[9], (N_EVEN, HEAD_DIM), 0.1),
        'ev_subln': gain(ks[10], (N_EVEN, 2 * HEAD_DIM)),
        'ev_w_out': nrm(ks[11], (N_EVEN, D_MIX, D_MODEL), D_MIX ** -0.5),
        'ev_ffn_norm': gain(ks[12], (N_EVEN, D_MODEL)),
        'ev_w_gate': nrm(ks[13], (N_EVEN, D_MODEL, D_FF_DENSE), D_MODEL ** -0.5),
        'ev_w_up': nrm(ks[14], (N_EVEN, D_MODEL, D_FF_DENSE), D_MODEL ** -0.5),
        'ev_w_down': nrm(ks[15], (N_EVEN, D_FF_DENSE, D_MODEL), D_FF_DENSE ** -0.5),
        'od_attn_norm': gain(ks[16], (N_ODD, D_MODEL)),
        'od_w_qkv': nrm(ks[17], (N_ODD, D_MODEL, 3 * N_HEADS_CHUNK * HEAD_DIM), D_MODEL ** -0.5),
        'od_q_norm': gain(ks[18], (N_ODD, HEAD_DIM)),
        'od_k_norm': gain(ks[19], (N_ODD, HEAD_DIM)),
        'od_rel_bias': nrm(ks[20], (N_ODD, N_HEADS_CHUNK, N_REL), 0.5),
        'od_w_out': nrm(ks[21], (N_ODD, N_HEADS_CHUNK * HEAD_DIM, D_MODEL), (N_HEADS_CHUNK * HEAD_DIM) ** -0.5),
        'od_ffn_norm': gain(ks[22], (N_ODD, D_MODEL)),
        'od_router': nrm(ks[23], (N_ODD, D_MODEL, N_EXPERTS), D_MODEL ** -0.5),
        'od_we_gate': nrm(ks[24], (N_ODD, N_EXPERTS, D_MODEL, D_FF_EXPERT), D_MODEL ** -0.5),
        'od_we_up': nrm(ks[25], (N_ODD, N_EXPERTS, D_MODEL, D_FF_EXPERT), D_MODEL ** -0.5),
        'od_we_down': nrm(ks[26], (N_ODD, N_EXPERTS, D_FF_EXPERT, D_MODEL), D_FF_EXPERT ** -0.5),
    }


def reference(x, positions, ev_attn_norm, ev_w_in, ev_q_norm, ev_k_norm, ev_lambda_q1, ev_lambda_k1,
              ev_lambda_q2, ev_lambda_k2, ev_subln, ev_w_out, ev_ffn_norm, ev_w_gate, ev_w_up, ev_w_down,
              od_attn_norm, od_w_qkv, od_q_norm, od_k_norm, od_rel_bias, od_w_out, od_ffn_norm, od_router,
              od_we_gate, od_we_up, od_we_down):
    cos, sin = rope_tables(positions)
    for layer in range(DEPTH):
        i = layer // 2
        if layer % 2 == 0:
            x = x + even_mixer(rms_norm(x, ev_attn_norm[i]), cos, sin, ev_w_in[i], ev_q_norm[i], ev_k_norm[i],
                               ev_lambda_q1[i], ev_lambda_k1[i], ev_lambda_q2[i], ev_lambda_k2[i],
                               ev_subln[i], ev_w_out[i], layer)
            x = x + swiglu(rms_norm(x, ev_ffn_norm[i]), ev_w_gate[i], ev_w_up[i], ev_w_down[i])
        else:
            x = x + odd_mixer(rms_norm(x, od_attn_norm[i]), od_w_qkv[i], od_q_norm[i], od_k_norm[i],
                              od_rel_bias[i], od_w_out[i])
            x = x + moe_swiglu(rms_norm(x, od_ffn_norm[i]), od_router[i], od_we_gate[i], od_we_up[i],
                               od_we_down[i])
    return x
```

```python
import functools
import math

import numpy as np
import jax
import jax.numpy as jnp
from jax import lax
from jax.experimental import pallas as pl
from jax.experimental.pallas import tpu as pltpu

F32 = jnp.float32
BF16 = jnp.bfloat16

HEAD_DIM = 64
CHUNK = 64
N_HEADS_SB = 8
N_HEADS_DIFF = 4
ROPE_THETA = 500000.0
ROT_DIM = HEAD_DIM // 4
LEFT_CHUNKS = 8
REL_CLIP = 128
N_EXPERTS = 8
TOP_K = 2
RMS_EPS = 1e-6
ATTN_SCALE = HEAD_DIM ** -0.5

LANES = 128
VMEM_LIMIT_BYTES = 48 * 1024 * 1024

SB_ZERO_LOG = -750.0

TM_PROJ = 512
TQ_SB = 256
TQ_DIFF = 512
TQ_BAND = 256
BAND_WINDOW = TQ_BAND + LEFT_CHUNKS * CHUNK
TF_DENSE = 1408
TF_EXPERT = 1792
TM_EXPERT = 512
TM_ROUTE = 512
TM_ROWS = 256

_NT = (((1,), (1,)), ((), ()))


def _params(*sem):
    return pltpu.CompilerParams(dimension_semantics=sem, vmem_limit_bytes=VMEM_LIMIT_BYTES)


def _rms(x, g):
    return x * lax.rsqrt(jnp.mean(x * x, axis=-1, keepdims=True) + RMS_EPS) * g


def _split_bf16(x):
    hi = x.astype(BF16)
    lo = (x - hi.astype(F32)).astype(BF16)
    return hi, lo


def _norm_proj_kernel(x_ref, g_ref, w_ref, o_ref, *, n_chunk):
    h = _rms(x_ref[...], g_ref[...]).astype(BF16)
    for c in range(0, o_ref.shape[1], n_chunk):
        o_ref[:, c:c + n_chunk] = jnp.dot(
            h, w_ref[:, c:c + n_chunk], preferred_element_type=F32).astype(o_ref.dtype)


def norm_proj(x, g, w):
    t, d = x.shape
    n = w.shape[1]
    tm = min(TM_PROJ, t)
    return pl.pallas_call(
        functools.partial(_norm_proj_kernel, n_chunk=512),
        grid=(t // tm,),
        in_specs=[pl.BlockSpec((tm, d), lambda i: (i, 0)),
                  pl.BlockSpec((1, d), lambda i: (0, 0)),
                  pl.BlockSpec((d, n), lambda i: (0, 0))],
        out_specs=pl.BlockSpec((tm, n), lambda i: (i, 0)),
        out_shape=jax.ShapeDtypeStruct((t, n), BF16),
        compiler_params=_params("parallel"),
        name="norm_proj",
    )(x, g.reshape(1, d), w)


def _head_mean_sq(x, g_ref):
    hi, lo = _split_bf16(x * x)
    g = g_ref[...]
    return jnp.dot(hi, g, preferred_element_type=F32) + jnp.dot(lo, g, preferred_element_type=F32)


def _qk_prep_kernel(*refs, rope):
    if rope:
        q_ref, k_ref, gq_ref, gk_ref, grp_ref, pos_ref, invf_ref, qo_ref, ko_ref = refs
        ang = pos_ref[...] * invf_ref[...]
        d = lax.broadcasted_iota(jnp.int32, (1, LANES), 1) % HEAD_DIM
        cos = jnp.cos(ang)
        sin = jnp.sin(ang)
        coef_fwd = jnp.where(d < ROT_DIM // 2, -sin, 0.0)
        coef_bwd = jnp.where((d >= ROT_DIM // 2) & (d < ROT_DIM), sin, 0.0)
    else:
        q_ref, k_ref, gq_ref, gk_ref, grp_ref, qo_ref, ko_ref = refs
    width = q_ref.shape[1]
    for src, gain, dst, scale in ((q_ref, gq_ref, qo_ref, ATTN_SCALE), (k_ref, gk_ref, ko_ref, 1.0)):
        for c in range(0, width, LANES):
            x = src[:, c:c + LANES].astype(F32)
            y = x * lax.rsqrt(_head_mean_sq(x, grp_ref) + RMS_EPS) * gain[...]
            if rope:
                half = ROT_DIM // 2
                y = (y * cos + pltpu.roll(y, LANES - half, 1) * coef_fwd
                     + pltpu.roll(y, half, 1) * coef_bwd)
            dst[:, c:c + LANES] = (y * scale).astype(dst.dtype)


def qk_prep(proj, q_col, k_col, width, gq, gk, pos=None):
    t = proj.shape[0]
    tm = min(TM_PROJ, t)
    rope = pos is not None
    grp = np.kron(np.eye(LANES // HEAD_DIM), np.full((HEAD_DIM, HEAD_DIM), 1.0 / HEAD_DIM))
    vec = lambda g: jnp.tile(g.astype(F32), LANES // HEAD_DIM).reshape(1, LANES)
    args = [proj, proj, vec(gq), vec(gk), jnp.asarray(grp, BF16)]
    const = lambda shape: pl.BlockSpec(shape, lambda i: (0, 0))
    in_specs = [pl.BlockSpec((tm, width), lambda i: (i, q_col)),
                pl.BlockSpec((tm, width), lambda i: (i, k_col)),
                const((1, LANES)), const((1, LANES)), const((LANES, LANES))]
    if rope:
        inv_freq = ROPE_THETA ** (-jnp.arange(0, ROT_DIM, 2, dtype=F32) / ROT_DIM)
        per_head = jnp.concatenate([inv_freq, inv_freq, jnp.zeros((HEAD_DIM - ROT_DIM,), F32)])
        args += [pos.astype(F32).reshape(t, 1), jnp.tile(per_head, LANES // HEAD_DIM).reshape(1, LANES)]
        in_specs += [pl.BlockSpec((tm, 1), lambda i: (i, 0)), const((1, LANES))]
    out = jax.ShapeDtypeStruct((t, width), BF16)
    return pl.pallas_call(
        functools.partial(_qk_prep_kernel, rope=rope),
        grid=(t // tm,),
        in_specs=in_specs,
        out_specs=[pl.BlockSpec((tm, width), lambda i: (i, 0))] * 2,
        out_shape=[out, out],
        compiler_params=_params("parallel"),
        name="qk_prep_rope" if rope else "qk_prep",
    )(*args)


def _sb_kernel(q_ref, k_ref, v_ref, o_ref, tri_ref, carry_ref, acc_ref, *, tq):
    i = pl.program_id(2)
    lane = lax.broadcasted_iota(jnp.int32, (1, LANES), 1)
    row = lax.broadcasted_iota(jnp.int32, (tq, tq), 0)
    col = lax.broadcasted_iota(jnp.int32, (tq, tq), 1)
    tri_ref[...] = jnp.where(row > col, 1.0, 0.0).astype(BF16)
    q = q_ref[...] * ATTN_SCALE

    def block(qh, j, diag):
        start = pl.multiple_of(j * tq, tq)
        k = k_ref[pl.ds(start, tq), :]
        v = v_ref[pl.ds(start, tq), :]
        z = lax.dot_general(qh, k, _NT, preferred_element_type=F32)
        softplus_tail = jnp.log1p(jnp.exp(-jnp.abs(z)))
        log_stay = -(jnp.maximum(z, 0.0) + softplus_tail)
        log_beta = jnp.minimum(z, 0.0) - softplus_tail
        if diag:
            past = col < row
            log_stay = jnp.where(past, log_stay, 0.0)
        hi, lo = _split_bf16(log_stay)
        tri = tri_ref[...]
        between = (jnp.dot(hi, tri, preferred_element_type=F32)
                   + jnp.dot(lo, tri, preferred_element_type=F32))
        w = jnp.exp(log_beta + between + carry_ref[...])
        if diag:
            w = jnp.where(past, w, 0.0)
        acc_ref[...] += jnp.dot(w.astype(BF16), v, preferred_element_type=F32)
        carry_ref[...] += jnp.sum(log_stay, axis=-1, keepdims=True)

    out = jnp.zeros((tq, LANES), F32)
    for half in range(LANES // HEAD_DIM):
        in_head = (lane // HEAD_DIM) == half
        qh = jnp.where(in_head, q, jnp.zeros_like(q))
        carry_ref[...] = jnp.zeros_like(carry_ref)
        acc_ref[...] = jnp.zeros_like(acc_ref)
        block(qh, i, True)

        def cond(state):
            j, live = state
            return jnp.logical_and(j >= 0, live > 0)

        def body(state, qh=qh):
            j, _ = state
            block(qh, j, False)
            live = (jnp.max(carry_ref[...]) > SB_ZERO_LOG).astype(jnp.int32)
            return j - 1, live

        lax.while_loop(cond, body, (i - 1, jnp.int32(1)))
        out = out + jnp.where(in_head, acc_ref[...], 0.0)
    o_ref[...] = out.astype(o_ref.dtype)


def sb_attention(proj, batch, seq):
    t = proj.shape[0]
    tq = min(TQ_SB, seq)
    nq = seq // tq
    n_pair = N_HEADS_SB * HEAD_DIM // LANES
    return pl.pallas_call(
        functools.partial(_sb_kernel, tq=tq),
        grid=(batch, n_pair, nq),
        in_specs=[pl.BlockSpec((tq, LANES), lambda b, p, i: (b * nq + i, p)),
                  pl.BlockSpec((seq, LANES), lambda b, p, i: (b, n_pair + p)),
                  pl.BlockSpec((seq, LANES), lambda b, p, i: (b, 2 * n_pair + p))],
        out_specs=pl.BlockSpec((tq, LANES), lambda b, p, i: (b * nq + i, p)),
        out_shape=jax.ShapeDtypeStruct((t, n_pair * LANES), BF16),
        scratch_shapes=[pltpu.VMEM((tq, tq), BF16), pltpu.VMEM((tq, 1), F32),
                        pltpu.VMEM((tq, LANES), F32)],
        compiler_params=_params("parallel", "parallel", "arbitrary"),
        name="sb_attention",
    )(proj, proj, proj)


def _diff_kernel(q_ref, k_ref, v_ref, lam_ref, subln_ref, o_ref, m_ref, l_ref, acc_ref, *, tq, lam_init):
    i = pl.program_id(2)
    lane = lax.broadcasted_iota(jnp.int32, (1, LANES), 1)
    q = q_ref[...]
    q_maps = [jnp.where((lane // HEAD_DIM) == m, q, jnp.zeros_like(q)) for m in range(2)]
    m_ref[...] = jnp.full_like(m_ref, -jnp.inf)
    l_ref[...] = jnp.zeros_like(l_ref)
    acc_ref[...] = jnp.zeros_like(acc_ref)

    def step(j, diag):
        start = pl.multiple_of(j * tq, tq)
        k = k_ref[pl.ds(start, tq), :]
        v = v_ref[pl.ds(start, tq), :]
        for m in range(2):
            s = lax.dot_general(q_maps[m], k, _NT, preferred_element_type=F32)
            if diag:
                row = lax.broadcasted_iota(jnp.int32, (tq, tq), 0)
                col = lax.broadcasted_iota(jnp.int32, (tq, tq), 1)
                s = jnp.where((col // CHUNK) <= (row // CHUNK), s, -jnp.inf)
            m_old = m_ref[m]
            m_new = jnp.maximum(m_old, jnp.max(s, axis=-1, keepdims=True))
            p = jnp.exp(s - m_new)
            alpha = jnp.exp(m_old - m_new)
            l_ref[m] = alpha * l_ref[m] + jnp.sum(p, axis=-1, keepdims=True)
            acc_ref[m] = alpha * acc_ref[m] + jnp.dot(p.astype(BF16), v, preferred_element_type=F32)
            m_ref[m] = m_new

    def body(j, c):
        step(j, False)
        return c

    lax.fori_loop(0, i, body, 0)
    step(i, True)

    lam_vecs = lam_ref[...]
    lam = (jnp.exp(jnp.sum(lam_vecs[0:1] * lam_vecs[1:2], axis=-1, keepdims=True))
           - jnp.exp(jnp.sum(lam_vecs[2:3] * lam_vecs[3:4], axis=-1, keepdims=True)) + lam_init)
    o = acc_ref[0] / l_ref[0] - lam * (acc_ref[1] / l_ref[1])
    o_ref[...] = (_rms(o, subln_ref[...]) * (1.0 - lam_init)).astype(o_ref.dtype)


def diff_attention(qd, kd, proj, lam_vecs, subln, lam_init, batch, seq):
    t = qd.shape[0]
    tq = min(TQ_DIFF, seq)
    nq = seq // tq
    v_col = (3 * N_HEADS_SB * HEAD_DIM + 2 * N_HEADS_DIFF * 2 * HEAD_DIM) // LANES
    return pl.pallas_call(
        functools.partial(_diff_kernel, tq=tq, lam_init=lam_init),
        grid=(batch, N_HEADS_DIFF, nq),
        in_specs=[pl.BlockSpec((tq, LANES), lambda b, h, i: (b * nq + i, h)),
                  pl.BlockSpec((seq, LANES), lambda b, h, i: (b, h)),
                  pl.BlockSpec((seq, LANES), lambda b, h, i: (b, v_col + h)),
                  pl.BlockSpec((4, HEAD_DIM), lambda b, h, i: (0, 0)),
                  pl.BlockSpec((1, LANES), lambda b, h, i: (0, 0))],
        out_specs=pl.BlockSpec((tq, LANES), lambda b, h, i: (b * nq + i, h)),
        out_shape=jax.ShapeDtypeStruct((t, N_HEADS_DIFF * LANES), BF16),
        scratch_shapes=[pltpu.VMEM((2, tq, 1), F32), pltpu.VMEM((2, tq, 1), F32),
                        pltpu.VMEM((2, tq, LANES), F32)],
        compiler_params=_params("parallel", "parallel", "arbitrary"),
        name="diff_attention",
    )(qd, kd, proj, lam_vecs, subln.reshape(1, LANES))


def _band_kernel(q_ref, k_ref, v_ref, bias_ref, o_ref, *, seq):
    lane = lax.broadcasted_iota(jnp.int32, (1, LANES), 1)
    left = BAND_WINDOW - TQ_BAND

    def tile(t0, k_start, n_keys):
        q = q_ref[pl.ds(t0, TQ_BAND), :]
        k = k_ref[pl.ds(k_start, n_keys), :]
        v = v_ref[pl.ds(k_start, n_keys), :]
        out = jnp.zeros((TQ_BAND, LANES), F32)
        for half in range(LANES // HEAD_DIM):
            in_head = (lane // HEAD_DIM) == half
            qh = jnp.where(in_head, q, jnp.zeros_like(q))
            s = lax.dot_general(qh, k, _NT, preferred_element_type=F32)
            s = s + bias_ref[half, :, BAND_WINDOW - n_keys:]
            p = jnp.exp(s - jnp.max(s, axis=-1, keepdims=True))
            o = jnp.dot(p.astype(BF16), v, preferred_element_type=F32)
            out = out + jnp.where(in_head, o / jnp.sum(p, axis=-1, keepdims=True), 0.0)
        o_ref[pl.ds(t0, TQ_BAND), :] = out.astype(o_ref.dtype)

    n_tiles = seq // TQ_BAND
    n_edge = min(left // TQ_BAND, n_tiles)
    for i in range(n_edge):
        tile(i * TQ_BAND, 0, (i + 1) * TQ_BAND)

    def body(i, c):
        t0 = pl.multiple_of(i * TQ_BAND, TQ_BAND)
        tile(t0, pl.multiple_of(t0 - left, TQ_BAND), BAND_WINDOW)
        return c

    lax.fori_loop(n_edge, n_tiles, body, 0)


def band_bias_table(rel_table):
    r = np.arange(TQ_BAND)[:, None]
    c = np.arange(BAND_WINDOW)[None, :]
    delta = (BAND_WINDOW - TQ_BAND) + r - c
    idx = np.clip(delta, -REL_CLIP, REL_CLIP) + REL_CLIP
    q_chunk, k_chunk = r // CHUNK, c // CHUNK
    in_band = (k_chunk >= q_chunk) & (k_chunk <= q_chunk + LEFT_CHUNKS)
    return jnp.where(jnp.asarray(in_band)[None], rel_table.astype(F32)[:, idx], -jnp.inf)


def band_attention(qn, kn, proj, bias, batch, seq):
    t, width = qn.shape
    n_pair = width // LANES
    seq_block = lambda col0: pl.BlockSpec((seq, LANES), lambda b, p: (b, col0 + p))
    return pl.pallas_call(
        functools.partial(_band_kernel, seq=seq),
        grid=(batch, n_pair),
        in_specs=[seq_block(0), seq_block(0), seq_block(2 * n_pair),
                  pl.BlockSpec((LANES // HEAD_DIM, TQ_BAND, BAND_WINDOW), lambda b, p: (p, 0, 0))],
        out_specs=seq_block(0),
        out_shape=jax.ShapeDtypeStruct((t, width), BF16),
        compiler_params=_params("parallel", "parallel"),
        name="band_attention",
    )(qn, kn, proj, bias)


def _out_proj_kernel(*refs):
    x_ref, *part_refs, w_ref, o_ref = refs
    acc = x_ref[...]
    row = 0
    for part in part_refs:
        d = part.shape[1]
        acc = acc + jnp.dot(part[...], w_ref[row:row + d, :], preferred_element_type=F32)
        row += d
    o_ref[...] = acc


def out_proj_residual(x, parts, w):
    t, d = x.shape
    tm = min(TM_PROJ, t)
    return pl.pallas_call(
        _out_proj_kernel,
        grid=(t // tm,),
        in_specs=([pl.BlockSpec((tm, d), lambda i: (i, 0))]
                  + [pl.BlockSpec((tm, p.shape[1]), lambda i: (i, 0)) for p in parts]
                  + [pl.BlockSpec(w.shape, lambda i: (0, 0))]),
        out_specs=pl.BlockSpec((tm, d), lambda i: (i, 0)),
        out_shape=jax.ShapeDtypeStruct((t, d), F32),
        compiler_params=_params("parallel"),
        name="out_proj_residual",
    )(x, *parts, w)


def _swiglu_kernel(x_ref, g_ref, wg_ref, wu_ref, wd_ref, o_ref, h_ref, acc_ref):
    j = pl.program_id(1)

    @pl.when(j == 0)
    def _():
        x = x_ref[...]
        h_ref[...] = _rms(x, g_ref[...]).astype(BF16)
        acc_ref[...] = x

    h = h_ref[...]
    gate = jnp.dot(h, wg_ref[...], preferred_element_type=F32)
    up = jnp.dot(h, wu_ref[...], preferred_element_type=F32)
    act = (gate * jax.nn.sigmoid(gate) * up).astype(BF16)
    acc_ref[...] += jnp.dot(act, wd_ref[...], preferred_element_type=F32)

    @pl.when(j == pl.num_programs(1) - 1)
    def _():
        o_ref[...] = acc_ref[...]


def swiglu_residual(x, g, wg, wu, wd):
    t, d = x.shape
    f = wg.shape[1]
    tm = min(TM_PROJ, t)
    tf = TF_DENSE if f % TF_DENSE == 0 else f
    return pl.pallas_call(
        _swiglu_kernel,
        grid=(t // tm, f // tf),
        in_specs=[pl.BlockSpec((tm, d), lambda i, j: (i, 0)),
                  pl.BlockSpec((1, d), lambda i, j: (0, 0)),
                  pl.BlockSpec((d, tf), lambda i, j: (0, j)),
                  pl.BlockSpec((d, tf), lambda i, j: (0, j)),
                  pl.BlockSpec((tf, d), lambda i, j: (j, 0))],
        out_specs=pl.BlockSpec((tm, d), lambda i, j: (i, 0)),
        out_shape=jax.ShapeDtypeStruct((t, d), F32),
        scratch_shapes=[pltpu.VMEM((tm, d), BF16), pltpu.VMEM((tm, d), F32)],
        compiler_params=_params("parallel", "arbitrary"),
        name="swiglu_residual",
    )(x, g.reshape(1, d), wg, wu, wd)


def _router_kernel(x_ref, g_ref, wr_ref, h_ref, pick_ref, gate_ref, count_ref, carry_ref, *, tm):
    i = pl.program_id(0)

    @pl.when(i == 0)
    def _():
        carry_ref[...] = jnp.zeros_like(carry_ref)

    h = _rms(x_ref[...], g_ref[...])
    h_ref[...] = h
    logits = lax.dot_general(wr_ref[...], h, _NT, preferred_element_type=F32,
                             precision=lax.Precision.HIGHEST)
    e_idx = lax.broadcasted_iota(jnp.int32, (N_EXPERTS, tm), 0)

    def top1(vals):
        best = jnp.max(vals, axis=0, keepdims=True)
        arg = jnp.min(jnp.where(vals == best, e_idx, N_EXPERTS), axis=0, keepdims=True)
        return best, arg

    v1, i1 = top1(logits)
    v2, i2 = top1(jnp.where(e_idx == i1, -jnp.inf, logits))
    ratio = jnp.exp(v2 - v1)
    g1 = 1.0 / (1.0 + ratio)
    g2 = ratio / (1.0 + ratio)

    picked = jnp.where((e_idx == i1) | (e_idx == i2), 1.0, 0.0)
    a = lax.broadcasted_iota(jnp.int32, (tm, tm), 0)
    b = lax.broadcasted_iota(jnp.int32, (tm, tm), 1)
    before = jnp.where(a < b, 1.0, 0.0).astype(BF16)
    rank = carry_ref[...] + jnp.dot(picked.astype(BF16), before, preferred_element_type=F32)
    rank = rank.astype(jnp.int32)
    r1 = jnp.sum(jnp.where(e_idx == i1, rank, 0), axis=0, keepdims=True)
    r2 = jnp.sum(jnp.where(e_idx == i2, rank, 0), axis=0, keepdims=True)
    carry_ref[...] += jnp.sum(picked, axis=1, keepdims=True)

    pick_ref[...] = jnp.zeros_like(pick_ref)
    gate_ref[...] = jnp.zeros_like(gate_ref)
    for r, val in enumerate((i1, i2, r1, r2)):
        pick_ref[r:r + 1, :] = val
    for r, val in enumerate((g1, g2)):
        gate_ref[r:r + 1, :] = val
    count_ref[...] = jnp.broadcast_to(carry_ref[...], count_ref.shape).astype(jnp.int32)


def router(x, g, w_router):
    t, d = x.shape
    tm = min(TM_ROUTE, t)
    return pl.pallas_call(
        functools.partial(_router_kernel, tm=tm),
        grid=(t // tm,),
        in_specs=[pl.BlockSpec((tm, d), lambda i: (i, 0)),
                  pl.BlockSpec((1, d), lambda i: (0, 0)),
                  pl.BlockSpec((N_EXPERTS, d), lambda i: (0, 0))],
        out_specs=[pl.BlockSpec((tm, d), lambda i: (i, 0)),
                   pl.BlockSpec((8, tm), lambda i: (0, i)),
                   pl.BlockSpec((8, tm), lambda i: (0, i)),
                   pl.BlockSpec((N_EXPERTS, LANES), lambda i: (0, 0))],
        out_shape=[jax.ShapeDtypeStruct((t, d), F32),
                   jax.ShapeDtypeStruct((8, t), jnp.int32),
                   jax.ShapeDtypeStruct((8, t), F32),
                   jax.ShapeDtypeStruct((N_EXPERTS, LANES), jnp.int32)],
        scratch_shapes=[pltpu.VMEM((N_EXPERTS, 1), F32)],
        compiler_params=_params("arbitrary"),
        name="router",
    )(x, g.reshape(1, d), w_router.T)


def _row_copy(src_ref, src_row, dst_ref, dst_row, sem):
    return pltpu.make_async_copy(src_ref.at[pl.ds(src_row, 1)], dst_ref.at[pl.ds(dst_row, 1)], sem)


def _dispatch_kernel(dest_ref, h_ref, init_ref, xs_ref, sem, *, tm):
    del init_ref

    def issue(t, c):
        for k in range(TOP_K):
            _row_copy(h_ref, t, xs_ref, dest_ref[0, k, t], sem).start()
        return c

    def drain(t, c):
        for k in range(TOP_K):
            _row_copy(h_ref, 0, xs_ref, 0, sem).wait()
        return c

    lax.fori_loop(0, tm, issue, 0)
    lax.fori_loop(0, tm, drain, 0)


def dispatch_rows(h, dest, n_rows):
    t, d = h.shape
    tm = min(TM_ROWS, t)
    return pl.pallas_call(
        functools.partial(_dispatch_kernel, tm=tm),
        grid=(t // tm,),
        in_specs=[pl.BlockSpec((1, TOP_K, tm), lambda i: (i, 0, 0), memory_space=pltpu.SMEM),
                  pl.BlockSpec((tm, d), lambda i: (i, 0)),
                  pl.BlockSpec(memory_space=pl.ANY)],
        out_specs=pl.BlockSpec(memory_space=pl.ANY),
        out_shape=jax.ShapeDtypeStruct((n_rows, d), h.dtype),
        scratch_shapes=[pltpu.SemaphoreType.DMA(())],
        input_output_aliases={2: 0},
        compiler_params=_params("arbitrary"),
        name="dispatch_rows",
    )(dest, h, jnp.zeros((n_rows, d), h.dtype))


def _combine_kernel(dest_ref, x_ref, gate_ref, ys_ref, o_ref, buf_ref, sem, *, tm):
    def issue(t, c):
        for k in range(TOP_K):
            _row_copy(ys_ref, dest_ref[0, k, t], buf_ref.at[k], t, sem).start()
        return c

    def drain(t, c):
        for k in range(TOP_K):
            _row_copy(ys_ref, 0, buf_ref.at[k], 0, sem).wait()
        return c

    lax.fori_loop(0, tm, issue, 0)
    lax.fori_loop(0, tm, drain, 0)
    gate = gate_ref[...]
    o_ref[...] = x_ref[...] + gate[:, 0:1] * buf_ref[0] + gate[:, 1:2] * buf_ref[1]


def combine_rows(x, gates, ys, dest):
    t, d = x.shape
    tm = min(TM_ROWS, t)
    return pl.pallas_call(
        functools.partial(_combine_kernel, tm=tm),
        grid=(t // tm,),
        in_specs=[pl.BlockSpec((1, TOP_K, tm), lambda i: (i, 0, 0), memory_space=pltpu.SMEM),
                  pl.BlockSpec((tm, d), lambda i: (i, 0)),
                  pl.BlockSpec((tm, TOP_K), lambda i: (i, 0)),
                  pl.BlockSpec(memory_space=pl.ANY)],
        out_specs=pl.BlockSpec((tm, d), lambda i: (i, 0)),
        out_shape=jax.ShapeDtypeStruct((t, d), F32),
        scratch_shapes=[pltpu.VMEM((TOP_K, tm, d), F32), pltpu.SemaphoreType.DMA(())],
        compiler_params=_params("arbitrary"),
        name="combine_rows",
    )(dest, x, gates, ys)


def _expert_ffn_kernel(expert_ref, live_ref, xs_ref, wg_ref, wu_ref, wd_ref, o_ref, h_ref, acc_ref):
    del expert_ref
    i, j = pl.program_id(0), pl.program_id(1)
    live = live_ref[i] > 0

    @pl.when(j == 0)
    def _():
        h_ref[...] = xs_ref[...].astype(BF16)
        acc_ref[...] = jnp.zeros_like(acc_ref)

    @pl.when(live)
    def _():
        h = h_ref[...]
        gate = jnp.dot(h, wg_ref[0], preferred_element_type=F32)
        up = jnp.dot(h, wu_ref[0], preferred_element_type=F32)
        act = (gate * jax.nn.sigmoid(gate) * up).astype(BF16)
        acc_ref[...] += jnp.dot(act, wd_ref[0], preferred_element_type=F32)

    @pl.when(j == pl.num_programs(1) - 1)
    def _():
        o_ref[...] = acc_ref[...]


def expert_ffn(xs, tile_expert, tile_live, wg, wu, wd):
    r, d = xs.shape
    f = wg.shape[2]
    tm = TM_EXPERT
    tf = TF_EXPERT if f % TF_EXPERT == 0 else f
    grid_spec = pltpu.PrefetchScalarGridSpec(
        num_scalar_prefetch=2,
        grid=(r // tm, f // tf),
        in_specs=[pl.BlockSpec((tm, d), lambda i, j, e, v: (i, 0)),
                  pl.BlockSpec((1, d, tf), lambda i, j, e, v: (e[i], 0, j)),
                  pl.BlockSpec((1, d, tf), lambda i, j, e, v: (e[i], 0, j)),
                  pl.BlockSpec((1, tf, d), lambda i, j, e, v: (e[i], j, 0))],
        out_specs=pl.BlockSpec((tm, d), lambda i, j, e, v: (i, 0)),
        scratch_shapes=[pltpu.VMEM((tm, d), BF16), pltpu.VMEM((tm, d), F32)])
    return pl.pallas_call(
        _expert_ffn_kernel,
        grid_spec=grid_spec,
        out_shape=jax.ShapeDtypeStruct((r, d), F32),
        compiler_params=_params("parallel", "arbitrary"),
        name="expert_ffn",
    )(tile_expert, tile_live, xs, wg, wu, wd)


def moe_residual(x, g, w_router, wg, wu, wd):
    t, d = x.shape
    tm = TM_EXPERT
    n_rows = TOP_K * t + N_EXPERTS * tm
    n_tiles = n_rows // tm
    h, picks, gates, counts = router(x, g, w_router)

    counts = counts[:, 0]
    padded = (counts + tm - 1) // tm * tm
    ends = jnp.cumsum(padded)
    starts = ends - padded
    experts, ranks = picks[0:TOP_K], picks[TOP_K:2 * TOP_K]
    dest = jnp.sum(jnp.where(experts[None] == jnp.arange(N_EXPERTS)[:, None, None],
                             starts[:, None, None], 0), axis=0) + ranks
    tile_start = jnp.arange(n_tiles, dtype=jnp.int32) * tm
    tile_expert = jnp.minimum(jnp.sum(tile_start[:, None] >= ends[None, :], axis=1),
                              N_EXPERTS - 1).astype(jnp.int32)
    tile_live = (tile_start < ends[-1]).astype(jnp.int32)

    tr = min(TM_ROWS, t)
    dest_tiles = dest.reshape(TOP_K, t // tr, tr).transpose(1, 0, 2)
    xs = dispatch_rows(h, dest_tiles, n_rows)
    ys = expert_ffn(xs, tile_expert, tile_live, wg, wu, wd)
    return combine_rows(x, gates[0:TOP_K].T, ys, dest_tiles)


def kernel(x, positions, ev_attn_norm, ev_w_in, ev_q_norm, ev_k_norm, ev_lambda_q1, ev_lambda_k1, ev_lambda_q2, ev_lambda_k2, ev_subln, ev_w_out, ev_ffn_norm, ev_w_gate, ev_w_up, ev_w_down, od_attn_norm, od_w_qkv, od_q_norm, od_k_norm, od_rel_bias, od_w_out, od_ffn_norm, od_router, od_we_gate, od_we_up, od_we_down):
    batch, seq, d_model = x.shape
    depth = ev_attn_norm.shape[0] + od_attn_norm.shape[0]
    bf = lambda w: w.astype(BF16)
    x = x.reshape(batch * seq, d_model)
    d_sb = N_HEADS_SB * HEAD_DIM
    d_diff = N_HEADS_DIFF * 2 * HEAD_DIM
    for layer in range(depth):
        i = layer // 2
        if layer % 2 == 0:
            proj = norm_proj(x, ev_attn_norm[i], bf(ev_w_in[i]))
            o_sb = sb_attention(proj, batch, seq)
            qd, kd = qk_prep(proj, 3 * d_sb // d_diff, 3 * d_sb // d_diff + 1, d_diff,
                             ev_q_norm[i], ev_k_norm[i], positions)
            lam_init = 0.8 - 0.6 * math.exp(-0.3 * layer)
            lam_vecs = jnp.stack([ev_lambda_q1[i], ev_lambda_k1[i], ev_lambda_q2[i], ev_lambda_k2[i]])
            o_df = diff_attention(qd, kd, proj, lam_vecs.astype(F32), ev_subln[i], lam_init, batch, seq)
            x = out_proj_residual(x, [o_sb, o_df], bf(ev_w_out[i]))
            x = swiglu_residual(x, ev_ffn_norm[i], bf(ev_w_gate[i]), bf(ev_w_up[i]), bf(ev_w_down[i]))
        else:
            proj = norm_proj(x, od_attn_norm[i], bf(od_w_qkv[i]))
            qn, kn = qk_prep(proj, 0, 1, d_model, od_q_norm[i], od_k_norm[i])
            o_band = band_attention(qn, kn, proj, band_bias_table(od_rel_bias[i]), batch, seq)
            x = out_proj_residual(x, [o_band], bf(od_w_out[i]))
            x = moe_residual(x, od_ffn_norm[i], od_router[i], bf(od_we_gate[i]), bf(od_we_up[i]),
                             bf(od_we_down[i]))
    return x.reshape(batch, seq, d_model)
```

```python
import functools
import math

import numpy as np
import jax
import jax.numpy as jnp
from jax import lax
from jax.experimental import pallas as pl
from jax.experimental.pallas import tpu as pltpu

F32 = jnp.float32
BF16 = jnp.bfloat16

HEAD_DIM = 64
CHUNK = 64
N_HEADS_SB = 8
N_HEADS_DIFF = 4
ROPE_THETA = 500000.0
ROT_DIM = HEAD_DIM // 4
LEFT_CHUNKS = 8
REL_CLIP = 128
N_EXPERTS = 8
TOP_K = 2
RMS_EPS = 1e-6
ATTN_SCALE = HEAD_DIM ** -0.5

LANES = 128
VMEM_LIMIT_BYTES = 48 * 1024 * 1024

SB_ZERO_COST = 160.0

LOG2E = math.log2(math.e)
FIXED_STABILISER_MAX = 60.0
BOUND_SLACK = 1.02

TM_PROJ = 512
TQ_SB = 256
TQ_DIFF = 512
TQ_BAND = 256
BAND_WINDOW = TQ_BAND + LEFT_CHUNKS * CHUNK
TF_DENSE = 1408
TF_EXPERT = 1792
TM_EXPERT = 512
TM_ROUTE = 512
TM_ROWS = 256
ROW_DMA_UNROLL = 8

_NT = (((1,), (1,)), ((), ()))


def _params(*sem):
    return pltpu.CompilerParams(dimension_semantics=sem, vmem_limit_bytes=VMEM_LIMIT_BYTES)


def _rms(x, g):
    return x * lax.rsqrt(jnp.mean(x * x, axis=-1, keepdims=True) + RMS_EPS) * g


def _split_bf16(x):
    hi = x.astype(BF16)
    lo = (x - hi.astype(F32)).astype(BF16)
    return hi, lo


def _lane_sums(p):
    part = p[:, 0:LANES]
    for c in range(LANES, p.shape[1], LANES):
        part = part + p[:, c:c + LANES]
    return part


def _norm_proj_kernel(x_ref, g_ref, w_ref, o_ref, *, n_chunk):
    h = _rms(x_ref[...], g_ref[...]).astype(BF16)
    for c in range(0, o_ref.shape[1], n_chunk):
        o_ref[:, c:c + n_chunk] = jnp.dot(
            h, w_ref[:, c:c + n_chunk], preferred_element_type=F32).astype(o_ref.dtype)


def norm_proj(x, g, w):
    t, d = x.shape
    n = w.shape[1]
    tm = min(TM_PROJ, t)
    return pl.pallas_call(
        functools.partial(_norm_proj_kernel, n_chunk=512),
        grid=(t // tm,),
        in_specs=[pl.BlockSpec((tm, d), lambda i: (i, 0)),
                  pl.BlockSpec((1, d), lambda i: (0, 0)),
                  pl.BlockSpec((d, n), lambda i: (0, 0))],
        out_specs=pl.BlockSpec((tm, n), lambda i: (i, 0)),
        out_shape=jax.ShapeDtypeStruct((t, n), BF16),
        compiler_params=_params("parallel"),
        name="norm_proj",
    )(x, g.reshape(1, d), w)


def _head_mean_sq(x, g_ref):
    hi, lo = _split_bf16(x * x)
    g = g_ref[...]
    return jnp.dot(hi, g, preferred_element_type=F32) + jnp.dot(lo, g, preferred_element_type=F32)


def _qk_prep_kernel(*refs, rope):
    if rope:
        q_ref, k_ref, gq_ref, gk_ref, grp_ref, pos_ref, invf_ref, qo_ref, ko_ref = refs
        ang = pos_ref[...] * invf_ref[...]
        d = lax.broadcasted_iota(jnp.int32, (1, LANES), 1) % HEAD_DIM
        cos = jnp.cos(ang)
        sin = jnp.sin(ang)
        coef_fwd = jnp.where(d < ROT_DIM // 2, -sin, 0.0)
        coef_bwd = jnp.where((d >= ROT_DIM // 2) & (d < ROT_DIM), sin, 0.0)
    else:
        q_ref, k_ref, gq_ref, gk_ref, grp_ref, qo_ref, ko_ref = refs
    width = q_ref.shape[1]
    for src, gain, dst, scale in ((q_ref, gq_ref, qo_ref, ATTN_SCALE * LOG2E), (k_ref, gk_ref, ko_ref, 1.0)):
        for c in range(0, width, LANES):
            x = src[:, c:c + LANES].astype(F32)
            y = x * lax.rsqrt(_head_mean_sq(x, grp_ref) + RMS_EPS) * gain[...]
            if rope:
                half = ROT_DIM // 2
                y = (y * cos + pltpu.roll(y, LANES - half, 1) * coef_fwd
                     + pltpu.roll(y, half, 1) * coef_bwd)
            dst[:, c:c + LANES] = (y * scale).astype(dst.dtype)


def qk_prep(proj, q_col, k_col, width, gq, gk, pos=None):
    t = proj.shape[0]
    tm = min(TM_PROJ, t)
    rope = pos is not None
    grp = np.kron(np.eye(LANES // HEAD_DIM), np.full((HEAD_DIM, HEAD_DIM), 1.0 / HEAD_DIM))
    vec = lambda g: jnp.tile(g.astype(F32), LANES // HEAD_DIM).reshape(1, LANES)
    args = [proj, proj, vec(gq), vec(gk), jnp.asarray(grp, BF16)]
    const = lambda shape: pl.BlockSpec(shape, lambda i: (0, 0))
    in_specs = [pl.BlockSpec((tm, width), lambda i: (i, q_col)),
                pl.BlockSpec((tm, width), lambda i: (i, k_col)),
                const((1, LANES)), const((1, LANES)), const((LANES, LANES))]
    if rope:
        inv_freq = ROPE_THETA ** (-jnp.arange(0, ROT_DIM, 2, dtype=F32) / ROT_DIM)
        per_head = jnp.concatenate([inv_freq, inv_freq, jnp.zeros((HEAD_DIM - ROT_DIM,), F32)])
        args += [pos.astype(F32).reshape(t, 1), jnp.tile(per_head, LANES // HEAD_DIM).reshape(1, LANES)]
        in_specs += [pl.BlockSpec((tm, 1), lambda i: (i, 0)), const((1, LANES))]
    out = jax.ShapeDtypeStruct((t, width), BF16)
    return pl.pallas_call(
        functools.partial(_qk_prep_kernel, rope=rope),
        grid=(t // tm,),
        in_specs=in_specs,
        out_specs=[pl.BlockSpec((tm, width), lambda i: (i, 0))] * 2,
        out_shape=[out, out],
        compiler_params=_params("parallel"),
        name="qk_prep_rope" if rope else "qk_prep",
    )(*args)


def _sb_kernel(q_ref, k_ref, v_ref, o_ref, tri_ref, cost_ref, acc_ref, *, tq):
    i = pl.program_id(2)
    n_heads = LANES // HEAD_DIM
    lane = lax.broadcasted_iota(jnp.int32, (1, LANES), 1)
    row = lax.broadcasted_iota(jnp.int32, (tq, tq), 0)
    col = lax.broadcasted_iota(jnp.int32, (tq, tq), 1)
    tri_ref[...] = jnp.where(row > col, 1.0, 0.0).astype(BF16)
    q = q_ref[...]
    q_all = jnp.concatenate(
        [jnp.where((lane // HEAD_DIM) == h, q, jnp.zeros_like(q)) for h in range(n_heads)], axis=0)
    cost_ref[...] = jnp.zeros_like(cost_ref)
    acc_ref[...] = jnp.zeros_like(acc_ref)

    def block(j, diag):
        start = pl.multiple_of(j * tq, tq)
        k = k_ref[pl.ds(start, tq), :]
        v = v_ref[pl.ds(start, tq), :]
        tri = tri_ref[...]
        z = lax.dot_general(q_all, k, _NT, preferred_element_type=F32)
        tail = jnp.log2(1.0 + jnp.exp2(-jnp.abs(z)))
        cost_stay = jnp.maximum(z, 0.0) + tail
        if diag:
            past = jnp.concatenate([col < row] * n_heads, axis=0)
            cost_stay = jnp.where(past, cost_stay, 0.0)
        hi, lo = _split_bf16(cost_stay)
        between = (jnp.dot(hi, tri, preferred_element_type=F32)
                   + jnp.dot(lo, tri, preferred_element_type=F32))
        w = jnp.exp2(jnp.minimum(z, 0.0) - tail - between - cost_ref[...])
        if diag:
            w = jnp.where(past, w, 0.0)
        acc_ref[...] += jnp.dot(w.astype(BF16), v, preferred_element_type=F32)
        cost_ref[...] += jnp.sum(_lane_sums(cost_stay), axis=-1, keepdims=True)

    block(i, True)

    def cond(state):
        j, live = state
        return jnp.logical_and(j >= 0, live > 0)

    def body(state):
        j, _ = state
        block(j, False)
        live = (jnp.min(cost_ref[...]) < SB_ZERO_COST).astype(jnp.int32)
        return j - 1, live

    lax.while_loop(cond, body, (i - 1, jnp.int32(1)))
    out = jnp.zeros((tq, LANES), F32)
    for h in range(n_heads):
        out = out + jnp.where((lane // HEAD_DIM) == h, acc_ref[h * tq:(h + 1) * tq, :], 0.0)
    o_ref[...] = out.astype(o_ref.dtype)


def sb_attention(proj, batch, seq):
    t = proj.shape[0]
    tq = min(TQ_SB, seq)
    nq = seq // tq
    n_pair = N_HEADS_SB * HEAD_DIM // LANES
    return pl.pallas_call(
        functools.partial(_sb_kernel, tq=tq),
        grid=(batch, n_pair, nq),
        in_specs=[pl.BlockSpec((tq, LANES), lambda b, p, i: (b * nq + i, p)),
                  pl.BlockSpec((seq, LANES), lambda b, p, i: (b, n_pair + p)),
                  pl.BlockSpec((seq, LANES), lambda b, p, i: (b, 2 * n_pair + p))],
        out_specs=pl.BlockSpec((tq, LANES), lambda b, p, i: (b * nq + i, p)),
        out_shape=jax.ShapeDtypeStruct((t, n_pair * LANES), BF16),
        scratch_shapes=[pltpu.VMEM((tq, tq), BF16), pltpu.VMEM((LANES // HEAD_DIM * tq, 1), F32),
                        pltpu.VMEM((LANES // HEAD_DIM * tq, LANES), F32)],
        compiler_params=_params("parallel", "parallel", "arbitrary"),
        name="sb_attention",
    )(proj, proj, proj)


def _diff_kernel(bound_ref, q_ref, k_ref, v_ref, lam_ref, subln_ref, o_ref, m_ref, l_ref, acc_ref, *,
                 tq, lam_init):
    i = pl.program_id(2)
    bound = bound_ref[0, 0]
    lane = lax.broadcasted_iota(jnp.int32, (1, LANES), 1)
    q = q_ref[...]
    q_maps = [jnp.where((lane // HEAD_DIM) == m, q, jnp.zeros_like(q)) for m in range(2)]
    l_ref[...] = jnp.zeros_like(l_ref)
    acc_ref[...] = jnp.zeros_like(acc_ref)

    def scores(m, k, diag):
        s = lax.dot_general(q_maps[m], k, _NT, preferred_element_type=F32)
        if diag:
            row = lax.broadcasted_iota(jnp.int32, (tq, tq), 0)
            col = lax.broadcasted_iota(jnp.int32, (tq, tq), 1)
            s = jnp.where((col // CHUNK) <= (row // CHUNK), s, -jnp.inf)
        return s

    def fixed_step(j, diag):
        start = pl.multiple_of(j * tq, tq)
        k = k_ref[pl.ds(start, tq), :]
        v = v_ref[pl.ds(start, tq), :]
        for m in range(2):
            p = jnp.exp2(scores(m, k, diag) - bound)
            l_ref[m] += _lane_sums(p)
            acc_ref[m] += jnp.dot(p.astype(BF16), v, preferred_element_type=F32)

    def online_step(j, diag):
        start = pl.multiple_of(j * tq, tq)
        k = k_ref[pl.ds(start, tq), :]
        v = v_ref[pl.ds(start, tq), :]
        for m in range(2):
            s = scores(m, k, diag)
            m_old = m_ref[m]
            m_new = jnp.maximum(m_old, jnp.max(s, axis=-1, keepdims=True))
            p = jnp.exp2(s - m_new)
            alpha = jnp.exp2(m_old - m_new)
            l_ref[m] = alpha * l_ref[m] + _lane_sums(p)
            acc_ref[m] = alpha * acc_ref[m] + jnp.dot(p.astype(BF16), v, preferred_element_type=F32)
            m_ref[m] = m_new

    def sweep(step):
        def body(j, c):
            step(j, False)
            return c
        lax.fori_loop(0, i, body, 0)
        step(i, True)

    @pl.when(bound <= FIXED_STABILISER_MAX)
    def _():
        sweep(fixed_step)

    @pl.when(jnp.logical_not(bound <= FIXED_STABILISER_MAX))
    def _():
        m_ref[...] = jnp.full_like(m_ref, -jnp.inf)
        sweep(online_step)

    lam_vecs = lam_ref[...]
    lam = (jnp.exp(jnp.sum(lam_vecs[0:1] * lam_vecs[1:2], axis=-1, keepdims=True))
           - jnp.exp(jnp.sum(lam_vecs[2:3] * lam_vecs[3:4], axis=-1, keepdims=True)) + lam_init)
    norm = [jnp.sum(l_ref[m], axis=-1, keepdims=True) for m in range(2)]
    o = acc_ref[0] / norm[0] - lam * (acc_ref[1] / norm[1])
    o_ref[...] = (_rms(o, subln_ref[...]) * (1.0 - lam_init)).astype(o_ref.dtype)


def diff_attention(qd, kd, proj, score_bound, lam_vecs, subln, lam_init, batch, seq):
    t = qd.shape[0]
    tq = min(TQ_DIFF, seq)
    nq = seq // tq
    v_col = (3 * N_HEADS_SB * HEAD_DIM + 2 * N_HEADS_DIFF * 2 * HEAD_DIM) // LANES
    return pl.pallas_call(
        functools.partial(_diff_kernel, tq=tq, lam_init=lam_init),
        grid=(batch, N_HEADS_DIFF, nq),
        in_specs=[pl.BlockSpec(memory_space=pltpu.SMEM),
                  pl.BlockSpec((tq, LANES), lambda b, h, i: (b * nq + i, h)),
                  pl.BlockSpec((seq, LANES), lambda b, h, i: (b, h)),
                  pl.BlockSpec((seq, LANES), lambda b, h, i: (b, v_col + h)),
                  pl.BlockSpec((4, HEAD_DIM), lambda b, h, i: (0, 0)),
                  pl.BlockSpec((1, LANES), lambda b, h, i: (0, 0))],
        out_specs=pl.BlockSpec((tq, LANES), lambda b, h, i: (b * nq + i, h)),
        out_shape=jax.ShapeDtypeStruct((t, N_HEADS_DIFF * LANES), BF16),
        scratch_shapes=[pltpu.VMEM((2, tq, 1), F32), pltpu.VMEM((2, tq, LANES), F32),
                        pltpu.VMEM((2, tq, LANES), F32)],
        compiler_params=_params("parallel", "parallel", "arbitrary"),
        name="diff_attention",
    )(score_bound.reshape(1, 1), qd, kd, proj, lam_vecs, subln.reshape(1, LANES))


def _band_kernel(bound_ref, q_ref, k_ref, v_ref, bias_ref, o_ref, *, seq):
    bound = bound_ref[0, 0]
    lane = lax.broadcasted_iota(jnp.int32, (1, LANES), 1)
    left = BAND_WINDOW - TQ_BAND

    def tile(t0, k_start, n_keys, fixed):
        q = q_ref[pl.ds(t0, TQ_BAND), :]
        k = k_ref[pl.ds(k_start, n_keys), :]
        v = v_ref[pl.ds(k_start, n_keys), :]
        out = jnp.zeros((TQ_BAND, LANES), F32)
        for half in range(LANES // HEAD_DIM):
            in_head = (lane // HEAD_DIM) == half
            qh = jnp.where(in_head, q, jnp.zeros_like(q))
            s = lax.dot_general(qh, k, _NT, preferred_element_type=F32)
            s = s + bias_ref[half, :, BAND_WINDOW - n_keys:]
            p = jnp.exp2(s - (bound if fixed else jnp.max(s, axis=-1, keepdims=True)))
            o = jnp.dot(p.astype(BF16), v, preferred_element_type=F32)
            norm = jnp.sum(_lane_sums(p), axis=-1, keepdims=True)
            out = out + jnp.where(in_head, o / norm, 0.0)
        o_ref[pl.ds(t0, TQ_BAND), :] = out.astype(o_ref.dtype)

    def sweep(fixed):
        n_tiles = seq // TQ_BAND
        n_edge = min(left // TQ_BAND, n_tiles)
        for i in range(n_edge):
            tile(i * TQ_BAND, 0, (i + 1) * TQ_BAND, fixed)

        def body(i, c):
            t0 = pl.multiple_of(i * TQ_BAND, TQ_BAND)
            tile(t0, pl.multiple_of(t0 - left, TQ_BAND), BAND_WINDOW, fixed)
            return c

        lax.fori_loop(n_edge, n_tiles, body, 0)

    @pl.when(bound <= FIXED_STABILISER_MAX)
    def _():
        sweep(True)

    @pl.when(jnp.logical_not(bound <= FIXED_STABILISER_MAX))
    def _():
        sweep(False)


def band_bias_table(rel_table):
    left = BAND_WINDOW - TQ_BAND
    period = TQ_BAND + BAND_WINDOW
    table = rel_table.astype(F32)
    n_hi = left - REL_CLIP
    n_lo = BAND_WINDOW - (left + REL_CLIP) - 1
    rep = lambda col, n: jnp.repeat(table[:, col:col + 1], n, axis=1)
    by_offset = jnp.concatenate([rep(2 * REL_CLIP, n_hi), table[:, ::-1], rep(0, n_lo),
                                 rep(2 * REL_CLIP, TQ_BAND)], axis=1)
    skew = jnp.tile(by_offset, (1, TQ_BAND))[:, :TQ_BAND * (period - 1)]
    bias = skew.reshape(-1, TQ_BAND, period - 1)[:, :, :BAND_WINDOW]
    q_chunk = np.arange(TQ_BAND)[:, None] // CHUNK
    k_chunk = np.arange(BAND_WINDOW)[None, :] // CHUNK
    in_band = (k_chunk >= q_chunk) & (k_chunk <= q_chunk + LEFT_CHUNKS)
    return jnp.where(jnp.asarray(in_band)[None], bias * LOG2E, -jnp.inf)


def score_bound(gq, gk, bias_max=0.0):
    qk = HEAD_DIM * ATTN_SCALE * jnp.max(jnp.abs(gq)) * jnp.max(jnp.abs(gk))
    return ((qk + bias_max) * (LOG2E * BOUND_SLACK)).astype(F32)


def band_attention(qn, kn, proj, bias, score_bound, batch, seq):
    t, width = qn.shape
    n_pair = width // LANES
    seq_block = lambda col0: pl.BlockSpec((seq, LANES), lambda b, p: (b, col0 + p))
    return pl.pallas_call(
        functools.partial(_band_kernel, seq=seq),
        grid=(batch, n_pair),
        in_specs=[pl.BlockSpec(memory_space=pltpu.SMEM),
                  seq_block(0), seq_block(0), seq_block(2 * n_pair),
                  pl.BlockSpec((LANES // HEAD_DIM, TQ_BAND, BAND_WINDOW), lambda b, p: (p, 0, 0))],
        out_specs=seq_block(0),
        out_shape=jax.ShapeDtypeStruct((t, width), BF16),
        compiler_params=_params("parallel", "parallel"),
        name="band_attention",
    )(score_bound.reshape(1, 1), qn, kn, proj, bias)


def _out_proj_kernel(*refs):
    x_ref, *part_refs, w_ref, o_ref = refs
    acc = x_ref[...]
    row = 0
    for part in part_refs:
        d = part.shape[1]
        acc = acc + jnp.dot(part[...], w_ref[row:row + d, :], preferred_element_type=F32)
        row += d
    o_ref[...] = acc


def out_proj_residual(x, parts, w):
    t, d = x.shape
    tm = min(TM_PROJ, t)
    return pl.pallas_call(
        _out_proj_kernel,
        grid=(t // tm,),
        in_specs=([pl.BlockSpec((tm, d), lambda i: (i, 0))]
                  + [pl.BlockSpec((tm, p.shape[1]), lambda i: (i, 0)) for p in parts]
                  + [pl.BlockSpec(w.shape, lambda i: (0, 0))]),
        out_specs=pl.BlockSpec((tm, d), lambda i: (i, 0)),
        out_shape=jax.ShapeDtypeStruct((t, d), F32),
        compiler_params=_params("parallel"),
        name="out_proj_residual",
    )(x, *parts, w)


def _swiglu_kernel(x_ref, g_ref, wg_ref, wu_ref, wd_ref, o_ref, h_ref, acc_ref):
    j = pl.program_id(1)

    @pl.when(j == 0)
    def _():
        x = x_ref[...]
        h_ref[...] = _rms(x, g_ref[...]).astype(BF16)
        acc_ref[...] = x

    h = h_ref[...]
    gate = jnp.dot(h, wg_ref[...], preferred_element_type=F32)
    up = jnp.dot(h, wu_ref[...], preferred_element_type=F32)
    act = (gate * jax.nn.sigmoid(gate) * up).astype(BF16)
    acc_ref[...] += jnp.dot(act, wd_ref[...], preferred_element_type=F32)

    @pl.when(j == pl.num_programs(1) - 1)
    def _():
        o_ref[...] = acc_ref[...]


def swiglu_residual(x, g, wg, wu, wd):
    t, d = x.shape
    f = wg.shape[1]
    tm = min(TM_PROJ, t)
    tf = TF_DENSE if f % TF_DENSE == 0 else f
    return pl.pallas_call(
        _swiglu_kernel,
        grid=(t // tm, f // tf),
        in_specs=[pl.BlockSpec((tm, d), lambda i, j: (i, 0)),
                  pl.BlockSpec((1, d), lambda i, j: (0, 0)),
                  pl.BlockSpec((d, tf), lambda i, j: (0, j)),
                  pl.BlockSpec((d, tf), lambda i, j: (0, j)),
                  pl.BlockSpec((tf, d), lambda i, j: (j, 0))],
        out_specs=pl.BlockSpec((tm, d), lambda i, j: (i, 0)),
        out_shape=jax.ShapeDtypeStruct((t, d), F32),
        scratch_shapes=[pltpu.VMEM((tm, d), BF16), pltpu.VMEM((tm, d), F32)],
        compiler_params=_params("parallel", "arbitrary"),
        name="swiglu_residual",
    )(x, g.reshape(1, d), wg, wu, wd)


def _router_kernel(x_ref, g_ref, wr_ref, h_ref, pick_ref, gate_ref, count_ref, carry_ref, *, tm):
    i = pl.program_id(0)

    @pl.when(i == 0)
    def _():
        carry_ref[...] = jnp.zeros_like(carry_ref)

    h = _rms(x_ref[...], g_ref[...])
    h_ref[...] = h
    logits = lax.dot_general(wr_ref[...], h, _NT, preferred_element_type=F32,
                             precision=lax.Precision.HIGHEST)
    e_idx = lax.broadcasted_iota(jnp.int32, (N_EXPERTS, tm), 0)

    def top1(vals):
        best = jnp.max(vals, axis=0, keepdims=True)
        arg = jnp.min(jnp.where(vals == best, e_idx, N_EXPERTS), axis=0, keepdims=True)
        return best, arg

    v1, i1 = top1(logits)
    v2, i2 = top1(jnp.where(e_idx == i1, -jnp.inf, logits))
    ratio = jnp.exp(v2 - v1)
    g1 = 1.0 / (1.0 + ratio)
    g2 = ratio / (1.0 + ratio)

    picked = jnp.where((e_idx == i1) | (e_idx == i2), 1.0, 0.0)
    a = lax.broadcasted_iota(jnp.int32, (tm, tm), 0)
    b = lax.broadcasted_iota(jnp.int32, (tm, tm), 1)
    before = jnp.where(a < b, 1.0, 0.0).astype(BF16)
    rank = carry_ref[...] + jnp.dot(picked.astype(BF16), before, preferred_element_type=F32)
    rank = rank.astype(jnp.int32)
    r1 = jnp.sum(jnp.where(e_idx == i1, rank, 0), axis=0, keepdims=True)
    r2 = jnp.sum(jnp.where(e_idx == i2, rank, 0), axis=0, keepdims=True)
    carry_ref[...] += jnp.sum(picked, axis=1, keepdims=True)

    pick_ref[...] = jnp.zeros_like(pick_ref)
    gate_ref[...] = jnp.zeros_like(gate_ref)
    for r, val in enumerate((i1, i2, r1, r2)):
        pick_ref[r:r + 1, :] = val
    for r, val in enumerate((g1, g2)):
        gate_ref[r:r + 1, :] = val
    count_ref[...] = jnp.broadcast_to(carry_ref[...], count_ref.shape).astype(jnp.int32)


def router(x, g, w_router):
    t, d = x.shape
    tm = min(TM_ROUTE, t)
    return pl.pallas_call(
        functools.partial(_router_kernel, tm=tm),
        grid=(t // tm,),
        in_specs=[pl.BlockSpec((tm, d), lambda i: (i, 0)),
                  pl.BlockSpec((1, d), lambda i: (0, 0)),
                  pl.BlockSpec((N_EXPERTS, d), lambda i: (0, 0))],
        out_specs=[pl.BlockSpec((tm, d), lambda i: (i, 0)),
                   pl.BlockSpec((8, tm), lambda i: (0, i)),
                   pl.BlockSpec((8, tm), lambda i: (0, i)),
                   pl.BlockSpec((N_EXPERTS, LANES), lambda i: (0, 0))],
        out_shape=[jax.ShapeDtypeStruct((t, d), F32),
                   jax.ShapeDtypeStruct((8, t), jnp.int32),
                   jax.ShapeDtypeStruct((8, t), F32),
                   jax.ShapeDtypeStruct((N_EXPERTS, LANES), jnp.int32)],
        scratch_shapes=[pltpu.VMEM((N_EXPERTS, 1), F32)],
        compiler_params=_params("arbitrary"),
        name="router",
    )(x, g.reshape(1, d), w_router.T)


def _row_copy(src_ref, src_row, dst_ref, dst_row, sem):
    return pltpu.make_async_copy(src_ref.at[pl.ds(src_row, 1)], dst_ref.at[pl.ds(dst_row, 1)], sem)


def _dispatch_kernel(dest_ref, h_ref, init_ref, xs_ref, sem, *, tm):
    del init_ref

    def issue(t, c):
        for k in range(TOP_K):
            _row_copy(h_ref, t, xs_ref, dest_ref[0, k, t], sem).start()
        return c

    lax.fori_loop(0, tm, issue, 0, unroll=ROW_DMA_UNROLL)
    for k in range(TOP_K):
        pltpu.make_async_copy(h_ref, xs_ref.at[pl.ds(0, tm)], sem).wait()


def dispatch_rows(h, dest, n_rows):
    t, d = h.shape
    tm = min(TM_ROWS, t)
    return pl.pallas_call(
        functools.partial(_dispatch_kernel, tm=tm),
        grid=(t // tm,),
        in_specs=[pl.BlockSpec((1, TOP_K, tm), lambda i: (i, 0, 0), memory_space=pltpu.SMEM),
                  pl.BlockSpec((tm, d), lambda i: (i, 0)),
                  pl.BlockSpec(memory_space=pl.ANY)],
        out_specs=pl.BlockSpec(memory_space=pl.ANY),
        out_shape=jax.ShapeDtypeStruct((n_rows, d), h.dtype),
        scratch_shapes=[pltpu.SemaphoreType.DMA(())],
        input_output_aliases={2: 0},
        compiler_params=_params("arbitrary"),
        name="dispatch_rows",
    )(dest, h, jnp.zeros((n_rows, d), h.dtype))


def _combine_kernel(dest_ref, x_ref, gate_ref, ys_ref, o_ref, buf_ref, sem, *, tm):
    def issue(t, c):
        for k in range(TOP_K):
            _row_copy(ys_ref, dest_ref[0, k, t], buf_ref.at[k], t, sem).start()
        return c

    lax.fori_loop(0, tm, issue, 0, unroll=ROW_DMA_UNROLL)
    for k in range(TOP_K):
        pltpu.make_async_copy(ys_ref.at[pl.ds(0, tm)], buf_ref.at[k], sem).wait()
    gate = gate_ref[...]
    o_ref[...] = x_ref[...] + gate[:, 0:1] * buf_ref[0] + gate[:, 1:2] * buf_ref[1]


def combine_rows(x, gates, ys, dest):
    t, d = x.shape
    tm = min(TM_ROWS, t)
    return pl.pallas_call(
        functools.partial(_combine_kernel, tm=tm),
        grid=(t // tm,),
        in_specs=[pl.BlockSpec((1, TOP_K, tm), lambda i: (i, 0, 0), memory_space=pltpu.SMEM),
                  pl.BlockSpec((tm, d), lambda i: (i, 0)),
                  pl.BlockSpec((tm, TOP_K), lambda i: (i, 0)),
                  pl.BlockSpec(memory_space=pl.ANY)],
        out_specs=pl.BlockSpec((tm, d), lambda i: (i, 0)),
        out_shape=jax.ShapeDtypeStruct((t, d), F32),
        scratch_shapes=[pltpu.VMEM((TOP_K, tm, d), F32), pltpu.SemaphoreType.DMA(())],
        compiler_params=_params("arbitrary"),
        name="combine_rows",
    )(dest, x, gates, ys)


def _expert_ffn_kernel(expert_ref, live_ref, xs_ref, wg_ref, wu_ref, wd_ref, o_ref, h_ref, acc_ref):
    del expert_ref
    i, j = pl.program_id(0), pl.program_id(1)
    live = live_ref[i] > 0

    @pl.when(j == 0)
    def _():
        h_ref[...] = xs_ref[...].astype(BF16)
        acc_ref[...] = jnp.zeros_like(acc_ref)

    @pl.when(live)
    def _():
        h = h_ref[...]
        gate = jnp.dot(h, wg_ref[0], preferred_element_type=F32)
        up = jnp.dot(h, wu_ref[0], preferred_element_type=F32)
        act = (gate * jax.nn.sigmoid(gate) * up).astype(BF16)
        acc_ref[...] += jnp.dot(act, wd_ref[0], preferred_element_type=F32)

    @pl.when(j == pl.num_programs(1) - 1)
    def _():
        o_ref[...] = acc_ref[...]


def expert_ffn(xs, tile_expert, tile_live, wg, wu, wd):
    r, d = xs.shape
    f = wg.shape[2]
    tm = TM_EXPERT
    tf = TF_EXPERT if f % TF_EXPERT == 0 else f
    grid_spec = pltpu.PrefetchScalarGridSpec(
        num_scalar_prefetch=2,
        grid=(r // tm, f // tf),
        in_specs=[pl.BlockSpec((tm, d), lambda i, j, e, v: (i, 0)),
                  pl.BlockSpec((1, d, tf), lambda i, j, e, v: (e[i], 0, j)),
                  pl.BlockSpec((1, d, tf), lambda i, j, e, v: (e[i], 0, j)),
                  pl.BlockSpec((1, tf, d), lambda i, j, e, v: (e[i], j, 0))],
        out_specs=pl.BlockSpec((tm, d), lambda i, j, e, v: (i, 0)),
        scratch_shapes=[pltpu.VMEM((tm, d), BF16), pltpu.VMEM((tm, d), F32)])
    return pl.pallas_call(
        _expert_ffn_kernel,
        grid_spec=grid_spec,
        out_shape=jax.ShapeDtypeStruct((r, d), F32),
        compiler_params=_params("parallel", "arbitrary"),
        name="expert_ffn",
    )(tile_expert, tile_live, xs, wg, wu, wd)


def moe_residual(x, g, w_router, wg, wu, wd):
    t, d = x.shape
    tm = TM_EXPERT
    n_rows = TOP_K * t + N_EXPERTS * tm
    n_tiles = n_rows // tm
    h, picks, gates, counts = router(x, g, w_router)

    counts = counts[:, 0]
    padded = (counts + tm - 1) // tm * tm
    ends = jnp.cumsum(padded)
    starts = ends - padded
    experts, ranks = picks[0:TOP_K], picks[TOP_K:2 * TOP_K]
    dest = jnp.sum(jnp.where(experts[None] == jnp.arange(N_EXPERTS)[:, None, None],
                             starts[:, None, None], 0), axis=0) + ranks
    tile_start = jnp.arange(n_tiles, dtype=jnp.int32) * tm
    tile_expert = jnp.minimum(jnp.sum(tile_start[:, None] >= ends[None, :], axis=1),
                              N_EXPERTS - 1).astype(jnp.int32)
    tile_live = (tile_start < ends[-1]).astype(jnp.int32)

    tr = min(TM_ROWS, t)
    dest_tiles = dest.reshape(TOP_K, t // tr, tr).transpose(1, 0, 2)
    xs = dispatch_rows(h, dest_tiles, n_rows)
    ys = expert_ffn(xs, tile_expert, tile_live, wg, wu, wd)
    return combine_rows(x, gates[0:TOP_K].T, ys, dest_tiles)


def kernel(x, positions, ev_attn_norm, ev_w_in, ev_q_norm, ev_k_norm, ev_lambda_q1, ev_lambda_k1, ev_lambda_q2, ev_lambda_k2, ev_subln, ev_w_out, ev_ffn_norm, ev_w_gate, ev_w_up, ev_w_down, od_attn_norm, od_w_qkv, od_q_norm, od_k_norm, od_rel_bias, od_w_out, od_ffn_norm, od_router, od_we_gate, od_we_up, od_we_down):
    batch, seq, d_model = x.shape
    depth = ev_attn_norm.shape[0] + od_attn_norm.shape[0]
    bf = lambda w: w.astype(BF16)
    x = x.reshape(batch * seq, d_model)
    d_sb = N_HEADS_SB * HEAD_DIM
    d_diff = N_HEADS_DIFF * 2 * HEAD_DIM
    for layer in range(depth):
        i = layer // 2
        if layer % 2 == 0:
            col_scale = jnp.where(jnp.arange(ev_w_in.shape[2]) < d_sb, ATTN_SCALE * LOG2E, 1.0)
            proj = norm_proj(x, ev_attn_norm[i], bf(ev_w_in[i] * col_scale))
            o_sb = sb_attention(proj, batch, seq)
            qd, kd = qk_prep(proj, 3 * d_sb // d_diff, 3 * d_sb // d_diff + 1, d_diff,
                             ev_q_norm[i], ev_k_norm[i], positions)
            lam_init = 0.8 - 0.6 * math.exp(-0.3 * layer)
            lam_vecs = jnp.stack([ev_lambda_q1[i], ev_lambda_k1[i], ev_lambda_q2[i], ev_lambda_k2[i]])
            o_df = diff_attention(qd, kd, proj, score_bound(ev_q_norm[i], ev_k_norm[i]),
                                  lam_vecs.astype(F32), ev_subln[i], lam_init, batch, seq)
            x = out_proj_residual(x, [o_sb, o_df], bf(ev_w_out[i]))
            x = swiglu_residual(x, ev_ffn_norm[i], bf(ev_w_gate[i]), bf(ev_w_up[i]), bf(ev_w_down[i]))
        else:
            proj = norm_proj(x, od_attn_norm[i], bf(od_w_qkv[i]))
            qn, kn = qk_prep(proj, 0, 1, d_model, od_q_norm[i], od_k_norm[i])
            bound = score_bound(od_q_norm[i], od_k_norm[i], jnp.max(jnp.abs(od_rel_bias[i])))
            o_band = band_attention(qn, kn, proj, band_bias_table(od_rel_bias[i]), bound, batch, seq)
            x = out_proj_residual(x, [o_band], bf(od_w_out[i]))
            x = moe_residual(x, od_ffn_norm[i], od_router[i], bf(od_we_gate[i]), bf(od_we_up[i]),
                             bf(od_we_down[i]))
    return x.reshape(batch, seq, d_model)
```

```python
import functools
import math

import numpy as np
import jax
import jax.numpy as jnp
from jax import lax
from jax.experimental import pallas as pl
from jax.experimental.pallas import tpu as pltpu

F32 = jnp.float32
BF16 = jnp.bfloat16

HEAD_DIM = 64
CHUNK = 64
N_HEADS_SB = 8
N_HEADS_DIFF = 4
ROPE_THETA = 500000.0
ROT_DIM = HEAD_DIM // 4
LEFT_CHUNKS = 8
REL_CLIP = 128
N_EXPERTS = 8
TOP_K = 2
RMS_EPS = 1e-6
ATTN_SCALE = HEAD_DIM ** -0.5

LANES = 128
VMEM_LIMIT_BYTES = 48 * 1024 * 1024

SB_ZERO_COST = 160.0

LOG2E = math.log2(math.e)
FIXED_STABILISER_MAX = 60.0
BOUND_SLACK = 1.02

TM_PROJ = 512
TQ_SB = 256
TQ_DIFF = 1024
TQ_BAND = 256
BAND_WINDOW = TQ_BAND + LEFT_CHUNKS * CHUNK
TF_DENSE = 1408
TF_EXPERT = 1792
TM_EXPERT = 512
TM_ROUTE = 512
TM_ROWS = 256
ROW_DMA_UNROLL = 8

_NT = (((1,), (1,)), ((), ()))


def _params(*sem):
    return pltpu.CompilerParams(dimension_semantics=sem, vmem_limit_bytes=VMEM_LIMIT_BYTES)


def _rms(x, g):
    return x * lax.rsqrt(jnp.mean(x * x, axis=-1, keepdims=True) + RMS_EPS) * g


def _split_bf16(x):
    hi = x.astype(BF16)
    lo = (x - hi.astype(F32)).astype(BF16)
    return hi, lo


def _lane_sums(p):
    part = p[:, 0:LANES]
    for c in range(LANES, p.shape[1], LANES):
        part = part + p[:, c:c + LANES]
    return part


def _norm_proj_kernel(x_ref, g_ref, w_ref, o_ref, *, n_chunk):
    h = _rms(x_ref[...], g_ref[...]).astype(BF16)
    for c in range(0, o_ref.shape[1], n_chunk):
        o_ref[:, c:c + n_chunk] = jnp.dot(
            h, w_ref[:, c:c + n_chunk], preferred_element_type=F32).astype(o_ref.dtype)


def norm_proj(x, g, w):
    t, d = x.shape
    n = w.shape[1]
    tm = min(TM_PROJ, t)
    return pl.pallas_call(
        functools.partial(_norm_proj_kernel, n_chunk=512),
        grid=(t // tm,),
        in_specs=[pl.BlockSpec((tm, d), lambda i: (i, 0)),
                  pl.BlockSpec((1, d), lambda i: (0, 0)),
                  pl.BlockSpec((d, n), lambda i: (0, 0))],
        out_specs=pl.BlockSpec((tm, n), lambda i: (i, 0)),
        out_shape=jax.ShapeDtypeStruct((t, n), BF16),
        compiler_params=_params("parallel"),
        name="norm_proj",
    )(x, g.reshape(1, d), w)


def _head_mean_sq(x, g_ref):
    hi, lo = _split_bf16(x * x)
    g = g_ref[...]
    return jnp.dot(hi, g, preferred_element_type=F32) + jnp.dot(lo, g, preferred_element_type=F32)


def _qk_prep_kernel(*refs, rope):
    if rope:
        q_ref, k_ref, gq_ref, gk_ref, grp_ref, pos_ref, invf_ref, qo_ref, ko_ref = refs
        ang = pos_ref[...] * invf_ref[...]
        d = lax.broadcasted_iota(jnp.int32, (1, LANES), 1) % HEAD_DIM
        cos = jnp.cos(ang)
        sin = jnp.sin(ang)
        coef_fwd = jnp.where(d < ROT_DIM // 2, -sin, 0.0)
        coef_bwd = jnp.where((d >= ROT_DIM // 2) & (d < ROT_DIM), sin, 0.0)
    else:
        q_ref, k_ref, gq_ref, gk_ref, grp_ref, qo_ref, ko_ref = refs
    width = q_ref.shape[1]
    for src, gain, dst, scale in ((q_ref, gq_ref, qo_ref, ATTN_SCALE * LOG2E), (k_ref, gk_ref, ko_ref, 1.0)):
        for c in range(0, width, LANES):
            x = src[:, c:c + LANES].astype(F32)
            y = x * lax.rsqrt(_head_mean_sq(x, grp_ref) + RMS_EPS) * gain[...]
            if rope:
                half = ROT_DIM // 2
                y = (y * cos + pltpu.roll(y, LANES - half, 1) * coef_fwd
                     + pltpu.roll(y, half, 1) * coef_bwd)
            dst[:, c:c + LANES] = (y * scale).astype(dst.dtype)


def qk_prep(proj, q_col, k_col, width, gq, gk, pos=None):
    t = proj.shape[0]
    tm = min(TM_PROJ, t)
    rope = pos is not None
    grp = np.kron(np.eye(LANES // HEAD_DIM), np.full((HEAD_DIM, HEAD_DIM), 1.0 / HEAD_DIM))
    vec = lambda g: jnp.tile(g.astype(F32), LANES // HEAD_DIM).reshape(1, LANES)
    args = [proj, proj, vec(gq), vec(gk), jnp.asarray(grp, BF16)]
    const = lambda shape: pl.BlockSpec(shape, lambda i: (0, 0))
    in_specs = [pl.BlockSpec((tm, width), lambda i: (i, q_col)),
                pl.BlockSpec((tm, width), lambda i: (i, k_col)),
                const((1, LANES)), const((1, LANES)), const((LANES, LANES))]
    if rope:
        inv_freq = ROPE_THETA ** (-jnp.arange(0, ROT_DIM, 2, dtype=F32) / ROT_DIM)
        per_head = jnp.concatenate([inv_freq, inv_freq, jnp.zeros((HEAD_DIM - ROT_DIM,), F32)])
        args += [pos.astype(F32).reshape(t, 1), jnp.tile(per_head, LANES // HEAD_DIM).reshape(1, LANES)]
        in_specs += [pl.BlockSpec((tm, 1), lambda i: (i, 0)), const((1, LANES))]
    out = jax.ShapeDtypeStruct((t, width), BF16)
    return pl.pallas_call(
        functools.partial(_qk_prep_kernel, rope=rope),
        grid=(t // tm,),
        in_specs=in_specs,
        out_specs=[pl.BlockSpec((tm, width), lambda i: (i, 0))] * 2,
        out_shape=[out, out],
        compiler_params=_params("parallel"),
        name="qk_prep_rope" if rope else "qk_prep",
    )(*args)


def _sb_kernel(q_ref, k_ref, v_ref, o_ref, tri_ref, cost_ref, acc_ref, *, tq):
    i = pl.program_id(2)
    n_heads = LANES // HEAD_DIM
    lane = lax.broadcasted_iota(jnp.int32, (1, LANES), 1)
    row = lax.broadcasted_iota(jnp.int32, (tq, tq), 0)
    col = lax.broadcasted_iota(jnp.int32, (tq, tq), 1)
    tri_ref[...] = jnp.where(row > col, 1.0, 0.0).astype(BF16)
    q = q_ref[...]
    q_all = jnp.concatenate(
        [jnp.where((lane // HEAD_DIM) == h, q, jnp.zeros_like(q)) for h in range(n_heads)], axis=0)
    cost_ref[...] = jnp.zeros_like(cost_ref)
    acc_ref[...] = jnp.zeros_like(acc_ref)

    def block(j, diag):
        start = pl.multiple_of(j * tq, tq)
        k = k_ref[pl.ds(start, tq), :]
        v = v_ref[pl.ds(start, tq), :]
        tri = tri_ref[...]
        z = lax.dot_general(q_all, k, _NT, preferred_element_type=F32)
        tail = jnp.log2(1.0 + jnp.exp2(-jnp.abs(z)))
        cost_stay = jnp.maximum(z, 0.0) + tail
        if diag:
            past = jnp.concatenate([col < row] * n_heads, axis=0)
            cost_stay = jnp.where(past, cost_stay, 0.0)
        hi, lo = _split_bf16(cost_stay)
        between = (jnp.dot(hi, tri, preferred_element_type=F32)
                   + jnp.dot(lo, tri, preferred_element_type=F32))
        w = jnp.exp2(jnp.minimum(z, 0.0) - tail - between - cost_ref[...])
        if diag:
            w = jnp.where(past, w, 0.0)
        acc_ref[...] += jnp.dot(w.astype(BF16), v, preferred_element_type=F32)
        cost_ref[...] += jnp.sum(_lane_sums(cost_stay), axis=-1, keepdims=True)

    block(i, True)

    def cond(state):
        j, live = state
        return jnp.logical_and(j >= 0, live > 0)

    def body(state):
        j, _ = state
        block(j, False)
        live = (jnp.min(cost_ref[...]) < SB_ZERO_COST).astype(jnp.int32)
        return j - 1, live

    lax.while_loop(cond, body, (i - 1, jnp.int32(1)))
    out = jnp.zeros((tq, LANES), F32)
    for h in range(n_heads):
        out = out + jnp.where((lane // HEAD_DIM) == h, acc_ref[h * tq:(h + 1) * tq, :], 0.0)
    o_ref[...] = out.astype(o_ref.dtype)


def sb_attention(proj, batch, seq):
    t = proj.shape[0]
    tq = min(TQ_SB, seq)
    nq = seq // tq
    n_pair = N_HEADS_SB * HEAD_DIM // LANES
    return pl.pallas_call(
        functools.partial(_sb_kernel, tq=tq),
        grid=(batch, n_pair, nq),
        in_specs=[pl.BlockSpec((tq, LANES), lambda b, p, i: (b * nq + i, p)),
                  pl.BlockSpec((seq, LANES), lambda b, p, i: (b, n_pair + p)),
                  pl.BlockSpec((seq, LANES), lambda b, p, i: (b, 2 * n_pair + p))],
        out_specs=pl.BlockSpec((tq, LANES), lambda b, p, i: (b * nq + i, p)),
        out_shape=jax.ShapeDtypeStruct((t, n_pair * LANES), BF16),
        scratch_shapes=[pltpu.VMEM((tq, tq), BF16), pltpu.VMEM((LANES // HEAD_DIM * tq, 1), F32),
                        pltpu.VMEM((LANES // HEAD_DIM * tq, LANES), F32)],
        compiler_params=_params("parallel", "parallel", "arbitrary"),
        name="sb_attention",
    )(proj, proj, proj)


def _diff_kernel(bound_ref, q_ref, k_ref, v_ref, lam_ref, subln_ref, o_ref, m_ref, l_ref, acc_ref, *,
                 tq, lam_init):
    i = pl.program_id(2)
    bound = bound_ref[0, 0]
    lane = lax.broadcasted_iota(jnp.int32, (1, LANES), 1)
    q = q_ref[...]
    q_maps = [jnp.where((lane // HEAD_DIM) == m, q, jnp.zeros_like(q)) for m in range(2)]
    l_ref[...] = jnp.zeros_like(l_ref)
    acc_ref[...] = jnp.zeros_like(acc_ref)

    def scores(m, k, diag, r0=0, c0=0):
        nr, nk = tq - r0, k.shape[0]
        s = lax.dot_general(q_maps[m][r0:], k, _NT, preferred_element_type=F32)
        if diag:
            row = r0 + lax.broadcasted_iota(jnp.int32, (nr, nk), 0)
            col = c0 + lax.broadcasted_iota(jnp.int32, (nr, nk), 1)
            s = jnp.where((col // CHUNK) <= (row // CHUNK), s, -jnp.inf)
        return s

    def fixed_part(j, diag, r0, c0, nk):
        start = pl.multiple_of(j * tq + c0, nk)
        k = k_ref[pl.ds(start, nk), :]
        v = v_ref[pl.ds(start, nk), :]
        for m in range(2):
            p = jnp.exp2(scores(m, k, diag, r0, c0) - bound)
            l_ref[m, r0:] += _lane_sums(p)
            acc_ref[m, r0:] += jnp.dot(p.astype(BF16), v, preferred_element_type=F32)

    def fixed_step(j, diag):
        if diag:
            fixed_part(j, True, 0, 0, tq // 2)
            fixed_part(j, True, tq // 2, tq // 2, tq // 2)
        else:
            fixed_part(j, False, 0, 0, tq)

    def online_step(j, diag):
        start = pl.multiple_of(j * tq, tq)
        k = k_ref[pl.ds(start, tq), :]
        v = v_ref[pl.ds(start, tq), :]
        for m in range(2):
            s = scores(m, k, diag)
            m_old = m_ref[m]
            m_new = jnp.maximum(m_old, jnp.max(s, axis=-1, keepdims=True))
            p = jnp.exp2(s - m_new)
            alpha = jnp.exp2(m_old - m_new)
            l_ref[m] = alpha * l_ref[m] + _lane_sums(p)
            acc_ref[m] = alpha * acc_ref[m] + jnp.dot(p.astype(BF16), v, preferred_element_type=F32)
            m_ref[m] = m_new

    def sweep(step):
        def body(j, c):
            step(j, False)
            return c
        lax.fori_loop(0, i, body, 0)
        step(i, True)

    @pl.when(bound <= FIXED_STABILISER_MAX)
    def _():
        sweep(fixed_step)

    @pl.when(jnp.logical_not(bound <= FIXED_STABILISER_MAX))
    def _():
        m_ref[...] = jnp.full_like(m_ref, -jnp.inf)
        sweep(online_step)

    lam_vecs = lam_ref[...]
    lam = (jnp.exp(jnp.sum(lam_vecs[0:1] * lam_vecs[1:2], axis=-1, keepdims=True))
           - jnp.exp(jnp.sum(lam_vecs[2:3] * lam_vecs[3:4], axis=-1, keepdims=True)) + lam_init)
    norm = [jnp.sum(l_ref[m], axis=-1, keepdims=True) for m in range(2)]
    o = acc_ref[0] / norm[0] - lam * (acc_ref[1] / norm[1])
    o_ref[...] = (_rms(o, subln_ref[...]) * (1.0 - lam_init)).astype(o_ref.dtype)


def diff_attention(qd, kd, proj, score_bound, lam_vecs, subln, lam_init, batch, seq):
    t = qd.shape[0]
    tq = min(TQ_DIFF, seq)
    nq = seq // tq
    v_col = (3 * N_HEADS_SB * HEAD_DIM + 2 * N_HEADS_DIFF * 2 * HEAD_DIM) // LANES
    return pl.pallas_call(
        functools.partial(_diff_kernel, tq=tq, lam_init=lam_init),
        grid=(batch, N_HEADS_DIFF, nq),
        in_specs=[pl.BlockSpec(memory_space=pltpu.SMEM),
                  pl.BlockSpec((tq, LANES), lambda b, h, i: (b * nq + i, h)),
                  pl.BlockSpec((seq, LANES), lambda b, h, i: (b, h)),
                  pl.BlockSpec((seq, LANES), lambda b, h, i: (b, v_col + h)),
                  pl.BlockSpec((4, HEAD_DIM), lambda b, h, i: (0, 0)),
                  pl.BlockSpec((1, LANES), lambda b, h, i: (0, 0))],
        out_specs=pl.BlockSpec((tq, LANES), lambda b, h, i: (b * nq + i, h)),
        out_shape=jax.ShapeDtypeStruct((t, N_HEADS_DIFF * LANES), BF16),
        scratch_shapes=[pltpu.VMEM((2, tq, 1), F32), pltpu.VMEM((2, tq, LANES), F32),
                        pltpu.VMEM((2, tq, LANES), F32)],
        compiler_params=_params("parallel", "parallel", "arbitrary"),
        name="diff_attention",
    )(score_bound.reshape(1, 1), qd, kd, proj, lam_vecs, subln.reshape(1, LANES))


def _band_kernel(bound_ref, q_ref, k_ref, v_ref, bias_ref, o_ref, *, seq):
    bound = bound_ref[0, 0]
    lane = lax.broadcasted_iota(jnp.int32, (1, LANES), 1)
    left = BAND_WINDOW - TQ_BAND

    def tile(t0, k_start, n_keys, fixed):
        q = q_ref[pl.ds(t0, TQ_BAND), :]
        k = k_ref[pl.ds(k_start, n_keys), :]
        v = v_ref[pl.ds(k_start, n_keys), :]
        out = jnp.zeros((TQ_BAND, LANES), F32)
        for half in range(LANES // HEAD_DIM):
            in_head = (lane // HEAD_DIM) == half
            qh = jnp.where(in_head, q, jnp.zeros_like(q))
            s = lax.dot_general(qh, k, _NT, preferred_element_type=F32)
            s = s + bias_ref[half, :, BAND_WINDOW - n_keys:]
            p = jnp.exp2(s - (bound if fixed else jnp.max(s, axis=-1, keepdims=True)))
            o = jnp.dot(p.astype(BF16), v, preferred_element_type=F32)
            norm = jnp.sum(_lane_sums(p), axis=-1, keepdims=True)
            out = out + jnp.where(in_head, o / norm, 0.0)
        o_ref[pl.ds(t0, TQ_BAND), :] = out.astype(o_ref.dtype)

    def sweep(fixed):
        n_tiles = seq // TQ_BAND
        n_edge = min(left // TQ_BAND, n_tiles)
        for i in range(n_edge):
            tile(i * TQ_BAND, 0, (i + 1) * TQ_BAND, fixed)

        def body(i, c):
            t0 = pl.multiple_of(i * TQ_BAND, TQ_BAND)
            tile(t0, pl.multiple_of(t0 - left, TQ_BAND), BAND_WINDOW, fixed)
            return c

        lax.fori_loop(n_edge, n_tiles, body, 0)

    @pl.when(bound <= FIXED_STABILISER_MAX)
    def _():
        sweep(True)

    @pl.when(jnp.logical_not(bound <= FIXED_STABILISER_MAX))
    def _():
        sweep(False)


def band_bias_table(rel_table):
    left = BAND_WINDOW - TQ_BAND
    period = TQ_BAND + BAND_WINDOW
    k = np.arange(period)
    k = np.where(k < BAND_WINDOW, k, k - period)
    by_offset = rel_table.astype(F32)[:, np.clip(left - k, -REL_CLIP, REL_CLIP) + REL_CLIP]
    skew = jnp.tile(by_offset, (1, TQ_BAND))[:, :TQ_BAND * (period - 1)]
    bias = skew.reshape(-1, TQ_BAND, period - 1)[:, :, :BAND_WINDOW]
    q_chunk = np.arange(TQ_BAND)[:, None] // CHUNK
    k_chunk = np.arange(BAND_WINDOW)[None, :] // CHUNK
    in_band = (k_chunk >= q_chunk) & (k_chunk <= q_chunk + LEFT_CHUNKS)
    return jnp.where(jnp.asarray(in_band)[None], bias * LOG2E, -jnp.inf)


def score_bound(gq, gk, bias_max=0.0):
    qk = HEAD_DIM * ATTN_SCALE * jnp.max(jnp.abs(gq)) * jnp.max(jnp.abs(gk))
    return ((qk + bias_max) * (LOG2E * BOUND_SLACK)).astype(F32)


def band_attention(qn, kn, proj, bias, score_bound, batch, seq):
    t, width = qn.shape
    n_pair = width // LANES
    seq_block = lambda col0: pl.BlockSpec((seq, LANES), lambda b, p: (b, col0 + p))
    return pl.pallas_call(
        functools.partial(_band_kernel, seq=seq),
        grid=(batch, n_pair),
        in_specs=[pl.BlockSpec(memory_space=pltpu.SMEM),
                  seq_block(0), seq_block(0), seq_block(2 * n_pair),
                  pl.BlockSpec((LANES // HEAD_DIM, TQ_BAND, BAND_WINDOW), lambda b, p: (p, 0, 0))],
        out_specs=seq_block(0),
        out_shape=jax.ShapeDtypeStruct((t, width), BF16),
        compiler_params=_params("parallel", "parallel"),
        name="band_attention",
    )(score_bound.reshape(1, 1), qn, kn, proj, bias)


def _out_proj_kernel(*refs):
    x_ref, *part_refs, w_ref, o_ref = refs
    acc = x_ref[...]
    row = 0
    for part in part_refs:
        d = part.shape[1]
        acc = acc + jnp.dot(part[...], w_ref[row:row + d, :], preferred_element_type=F32)
        row += d
    o_ref[...] = acc


def out_proj_residual(x, parts, w):
    t, d = x.shape
    tm = min(TM_PROJ, t)
    return pl.pallas_call(
        _out_proj_kernel,
        grid=(t // tm,),
        in_specs=([pl.BlockSpec((tm, d), lambda i: (i, 0))]
                  + [pl.BlockSpec((tm, p.shape[1]), lambda i: (i, 0)) for p in parts]
                  + [pl.BlockSpec(w.shape, lambda i: (0, 0))]),
        out_specs=pl.BlockSpec((tm, d), lambda i: (i, 0)),
        out_shape=jax.ShapeDtypeStruct((t, d), F32),
        compiler_params=_params("parallel"),
        name="out_proj_residual",
    )(x, *parts, w)


def _swiglu_kernel(x_ref, g_ref, wg_ref, wu_ref, wd_ref, o_ref, h_ref, acc_ref):
    j = pl.program_id(1)

    @pl.when(j == 0)
    def _():
        x = x_ref[...]
        h_ref[...] = _rms(x, g_ref[...]).astype(BF16)
        acc_ref[...] = x

    h = h_ref[...]
    gate = jnp.dot(h, wg_ref[...], preferred_element_type=F32)
    up = jnp.dot(h, wu_ref[...], preferred_element_type=F32)
    act = (gate * jax.nn.sigmoid(gate) * up).astype(BF16)
    acc_ref[...] += jnp.dot(act, wd_ref[...], preferred_element_type=F32)

    @pl.when(j == pl.num_programs(1) - 1)
    def _():
        o_ref[...] = acc_ref[...]


def swiglu_residual(x, g, wg, wu, wd):
    t, d = x.shape
    f = wg.shape[1]
    tm = min(TM_PROJ, t)
    tf = TF_DENSE if f % TF_DENSE == 0 else f
    return pl.pallas_call(
        _swiglu_kernel,
        grid=(t // tm, f // tf),
        in_specs=[pl.BlockSpec((tm, d), lambda i, j: (i, 0)),
                  pl.BlockSpec((1, d), lambda i, j: (0, 0)),
                  pl.BlockSpec((d, tf), lambda i, j: (0, j)),
                  pl.BlockSpec((d, tf), lambda i, j: (0, j)),
                  pl.BlockSpec((tf, d), lambda i, j: (j, 0))],
        out_specs=pl.BlockSpec((tm, d), lambda i, j: (i, 0)),
        out_shape=jax.ShapeDtypeStruct((t, d), F32),
        scratch_shapes=[pltpu.VMEM((tm, d), BF16), pltpu.VMEM((tm, d), F32)],
        compiler_params=_params("parallel", "arbitrary"),
        name="swiglu_residual",
    )(x, g.reshape(1, d), wg, wu, wd)


def _router_kernel(x_ref, g_ref, wr_ref, h_ref, pick_ref, gate_ref, count_ref, carry_ref, *, tm):
    i = pl.program_id(0)

    @pl.when(i == 0)
    def _():
        carry_ref[...] = jnp.zeros_like(carry_ref)

    h = _rms(x_ref[...], g_ref[...])
    h_ref[...] = h
    logits = lax.dot_general(wr_ref[...], h, _NT, preferred_element_type=F32,
                             precision=lax.Precision.HIGHEST)
    e_idx = lax.broadcasted_iota(jnp.int32, (N_EXPERTS, tm), 0)

    def top1(vals):
        best = jnp.max(vals, axis=0, keepdims=True)
        arg = jnp.min(jnp.where(vals == best, e_idx, N_EXPERTS), axis=0, keepdims=True)
        return best, arg

    v1, i1 = top1(logits)
    v2, i2 = top1(jnp.where(e_idx == i1, -jnp.inf, logits))
    ratio = jnp.exp(v2 - v1)
    g1 = 1.0 / (1.0 + ratio)
    g2 = ratio / (1.0 + ratio)

    picked = jnp.where((e_idx == i1) | (e_idx == i2), 1.0, 0.0)
    a = lax.broadcasted_iota(jnp.int32, (tm, tm), 0)
    b = lax.broadcasted_iota(jnp.int32, (tm, tm), 1)
    before = jnp.where(a < b, 1.0, 0.0).astype(BF16)
    rank = carry_ref[...] + jnp.dot(picked.astype(BF16), before, preferred_element_type=F32)
    rank = rank.astype(jnp.int32)
    r1 = jnp.sum(jnp.where(e_idx == i1, rank, 0), axis=0, keepdims=True)
    r2 = jnp.sum(jnp.where(e_idx == i2, rank, 0), axis=0, keepdims=True)
    carry_ref[...] += jnp.sum(picked, axis=1, keepdims=True)

    pick_ref[...] = jnp.zeros_like(pick_ref)
    gate_ref[...] = jnp.zeros_like(gate_ref)
    for r, val in enumerate((i1, i2, r1, r2)):
        pick_ref[r:r + 1, :] = val
    for r, val in enumerate((g1, g2)):
        gate_ref[r:r + 1, :] = val
    count_ref[...] = jnp.broadcast_to(carry_ref[...], count_ref.shape).astype(jnp.int32)


def router(x, g, w_router):
    t, d = x.shape
    tm = min(TM_ROUTE, t)
    return pl.pallas_call(
        functools.partial(_router_kernel, tm=tm),
        grid=(t // tm,),
        in_specs=[pl.BlockSpec((tm, d), lambda i: (i, 0)),
                  pl.BlockSpec((1, d), lambda i: (0, 0)),
                  pl.BlockSpec((N_EXPERTS, d), lambda i: (0, 0))],
        out_specs=[pl.BlockSpec((tm, d), lambda i: (i, 0)),
                   pl.BlockSpec((8, tm), lambda i: (0, i)),
                   pl.BlockSpec((8, tm), lambda i: (0, i)),
                   pl.BlockSpec((N_EXPERTS, LANES), lambda i: (0, 0))],
        out_shape=[jax.ShapeDtypeStruct((t, d), F32),
                   jax.ShapeDtypeStruct((8, t), jnp.int32),
                   jax.ShapeDtypeStruct((8, t), F32),
                   jax.ShapeDtypeStruct((N_EXPERTS, LANES), jnp.int32)],
        scratch_shapes=[pltpu.VMEM((N_EXPERTS, 1), F32)],
        compiler_params=_params("arbitrary"),
        name="router",
    )(x, g.reshape(1, d), w_router.T)


def _row_copy(src_ref, src_row, dst_ref, dst_row, sem):
    return pltpu.make_async_copy(src_ref.at[pl.ds(src_row, 1)], dst_ref.at[pl.ds(dst_row, 1)], sem)


def _dispatch_kernel(dest_ref, h_ref, init_ref, xs_ref, sem, *, tm):
    del init_ref

    def issue(t, c):
        for k in range(TOP_K):
            _row_copy(h_ref, t, xs_ref, dest_ref[0, k, t], sem).start()
        return c

    lax.fori_loop(0, tm, issue, 0, unroll=ROW_DMA_UNROLL)
    for k in range(TOP_K):
        pltpu.make_async_copy(h_ref, xs_ref.at[pl.ds(0, tm)], sem).wait()


def dispatch_rows(h, dest, n_rows):
    t, d = h.shape
    tm = min(TM_ROWS, t)
    return pl.pallas_call(
        functools.partial(_dispatch_kernel, tm=tm),
        grid=(t // tm,),
        in_specs=[pl.BlockSpec((1, TOP_K, tm), lambda i: (i, 0, 0), memory_space=pltpu.SMEM),
                  pl.BlockSpec((tm, d), lambda i: (i, 0)),
                  pl.BlockSpec(memory_space=pl.ANY)],
        out_specs=pl.BlockSpec(memory_space=pl.ANY),
        out_shape=jax.ShapeDtypeStruct((n_rows, d), h.dtype),
        scratch_shapes=[pltpu.SemaphoreType.DMA(())],
        input_output_aliases={2: 0},
        compiler_params=_params("arbitrary"),
        name="dispatch_rows",
    )(dest, h, jnp.zeros((n_rows, d), h.dtype))


def _combine_kernel(dest_ref, x_ref, gate_ref, ys_ref, o_ref, buf_ref, sem, *, tm):
    def issue(t, c):
        for k in range(TOP_K):
            _row_copy(ys_ref, dest_ref[0, k, t], buf_ref.at[k], t, sem).start()
        return c

    lax.fori_loop(0, tm, issue, 0, unroll=ROW_DMA_UNROLL)
    for k in range(TOP_K):
        pltpu.make_async_copy(ys_ref.at[pl.ds(0, tm)], buf_ref.at[k], sem).wait()
    gate = gate_ref[...]
    o_ref[...] = x_ref[...] + gate[:, 0:1] * buf_ref[0] + gate[:, 1:2] * buf_ref[1]


def combine_rows(x, gates, ys, dest):
    t, d = x.shape
    tm = min(TM_ROWS, t)
    return pl.pallas_call(
        functools.partial(_combine_kernel, tm=tm),
        grid=(t // tm,),
        in_specs=[pl.BlockSpec((1, TOP_K, tm), lambda i: (i, 0, 0), memory_space=pltpu.SMEM),
                  pl.BlockSpec((tm, d), lambda i: (i, 0)),
                  pl.BlockSpec((tm, TOP_K), lambda i: (i, 0)),
                  pl.BlockSpec(memory_space=pl.ANY)],
        out_specs=pl.BlockSpec((tm, d), lambda i: (i, 0)),
        out_shape=jax.ShapeDtypeStruct((t, d), F32),
        scratch_shapes=[pltpu.VMEM((TOP_K, tm, d), F32), pltpu.SemaphoreType.DMA(())],
        compiler_params=_params("arbitrary"),
        name="combine_rows",
    )(dest, x, gates, ys)


def _expert_ffn_kernel(expert_ref, live_ref, xs_ref, wg_ref, wu_ref, wd_ref, o_ref, h_ref, acc_ref):
    del expert_ref
    i, j = pl.program_id(0), pl.program_id(1)
    live = live_ref[i] > 0

    @pl.when(j == 0)
    def _():
        h_ref[...] = xs_ref[...].astype(BF16)
        acc_ref[...] = jnp.zeros_like(acc_ref)

    @pl.when(live)
    def _():
        h = h_ref[...]
        gate = jnp.dot(h, wg_ref[0], preferred_element_type=F32)
        up = jnp.dot(h, wu_ref[0], preferred_element_type=F32)
        act = (gate * jax.nn.sigmoid(gate) * up).astype(BF16)
        acc_ref[...] += jnp.dot(act, wd_ref[0], preferred_element_type=F32)

    @pl.when(j == pl.num_programs(1) - 1)
    def _():
        o_ref[...] = acc_ref[...]


def expert_ffn(xs, tile_expert, tile_live, wg, wu, wd):
    r, d = xs.shape
    f = wg.shape[2]
    tm = TM_EXPERT
    tf = TF_EXPERT if f % TF_EXPERT == 0 else f
    grid_spec = pltpu.PrefetchScalarGridSpec(
        num_scalar_prefetch=2,
        grid=(r // tm, f // tf),
        in_specs=[pl.BlockSpec((tm, d), lambda i, j, e, v: (i, 0)),
                  pl.BlockSpec((1, d, tf), lambda i, j, e, v: (e[i], 0, j)),
                  pl.BlockSpec((1, d, tf), lambda i, j, e, v: (e[i], 0, j)),
                  pl.BlockSpec((1, tf, d), lambda i, j, e, v: (e[i], j, 0))],
        out_specs=pl.BlockSpec((tm, d), lambda i, j, e, v: (i, 0)),
        scratch_shapes=[pltpu.VMEM((tm, d), BF16), pltpu.VMEM((tm, d), F32)])
    return pl.pallas_call(
        _expert_ffn_kernel,
        grid_spec=grid_spec,
        out_shape=jax.ShapeDtypeStruct((r, d), F32),
        compiler_params=_params("parallel", "arbitrary"),
        name="expert_ffn",
    )(tile_expert, tile_live, xs, wg, wu, wd)


def moe_residual(x, g, w_router, wg, wu, wd):
    t, d = x.shape
    tm = TM_EXPERT
    n_rows = TOP_K * t + N_EXPERTS * tm
    n_tiles = n_rows // tm
    h, picks, gates, counts = router(x, g, w_router)

    counts = counts[:, 0]
    padded = (counts + tm - 1) // tm * tm
    ends = jnp.cumsum(padded)
    starts = ends - padded
    experts, ranks = picks[0:TOP_K], picks[TOP_K:2 * TOP_K]
    dest = jnp.sum(jnp.where(experts[None] == jnp.arange(N_EXPERTS)[:, None, None],
                             starts[:, None, None], 0), axis=0) + ranks
    tile_start = jnp.arange(n_tiles, dtype=jnp.int32) * tm
    tile_expert = jnp.minimum(jnp.sum(tile_start[:, None] >= ends[None, :], axis=1),
                              N_EXPERTS - 1).astype(jnp.int32)
    tile_live = (tile_start < ends[-1]).astype(jnp.int32)

    tr = min(TM_ROWS, t)
    dest_tiles = dest.reshape(TOP_K, t // tr, tr).transpose(1, 0, 2)
    xs = dispatch_rows(h, dest_tiles, n_rows)
    ys = expert_ffn(xs, tile_expert, tile_live, wg, wu, wd)
    return combine_rows(x, gates[0:TOP_K].T, ys, dest_tiles)


def kernel(x, positions, ev_attn_norm, ev_w_in, ev_q_norm, ev_k_norm, ev_lambda_q1, ev_lambda_k1, ev_lambda_q2, ev_lambda_k2, ev_subln, ev_w_out, ev_ffn_norm, ev_w_gate, ev_w_up, ev_w_down, od_attn_norm, od_w_qkv, od_q_norm, od_k_norm, od_rel_bias, od_w_out, od_ffn_norm, od_router, od_we_gate, od_we_up, od_we_down):
    batch, seq, d_model = x.shape
    depth = ev_attn_norm.shape[0] + od_attn_norm.shape[0]
    bf = lambda w: w.astype(BF16)
    x = x.reshape(batch * seq, d_model)
    d_sb = N_HEADS_SB * HEAD_DIM
    d_diff = N_HEADS_DIFF * 2 * HEAD_DIM
    for layer in range(depth):
        i = layer // 2
        if layer % 2 == 0:
            col_scale = jnp.where(jnp.arange(ev_w_in.shape[2]) < d_sb, ATTN_SCALE * LOG2E, 1.0)
            proj = norm_proj(x, ev_attn_norm[i], bf(ev_w_in[i] * col_scale))
            o_sb = sb_attention(proj, batch, seq)
            qd, kd = qk_prep(proj, 3 * d_sb // d_diff, 3 * d_sb // d_diff + 1, d_diff,
                             ev_q_norm[i], ev_k_norm[i], positions)
            lam_init = 0.8 - 0.6 * math.exp(-0.3 * layer)
            lam_vecs = jnp.stack([ev_lambda_q1[i], ev_lambda_k1[i], ev_lambda_q2[i], ev_lambda_k2[i]])
            o_df = diff_attention(qd, kd, proj, score_bound(ev_q_norm[i], ev_k_norm[i]),
                                  lam_vecs.astype(F32), ev_subln[i], lam_init, batch, seq)
            x = out_proj_residual(x, [o_sb, o_df], bf(ev_w_out[i]))
            x = swiglu_residual(x, ev_ffn_norm[i], bf(ev_w_gate[i]), bf(ev_w_up[i]), bf(ev_w_down[i]))
        else:
            proj = norm_proj(x, od_attn_norm[i], bf(od_w_qkv[i]))
            qn, kn = qk_prep(proj, 0, 1, d_model, od_q_norm[i], od_k_norm[i])
            bound = score_bound(od_q_norm[i], od_k_norm[i], jnp.max(jnp.abs(od_rel_bias[i])))
            o_band = band_attention(qn, kn, proj, band_bias_table(od_rel_bias[i]), bound, batch, seq)
            x = out_proj_residual(x, [o_band], bf(od_w_out[i]))
            x = moe_residual(x, od_ffn_norm[i], od_router[i], bf(od_we_gate[i]), bf(od_we_up[i]),
                             bf(od_we_down[i]))
    return x.reshape(batch, seq, d_model)
```

```python
import functools
import math

import numpy as np
import jax
import jax.numpy as jnp
from jax import lax
from jax.experimental import pallas as pl
from jax.experimental.pallas import tpu as pltpu

F32 = jnp.float32
BF16 = jnp.bfloat16

HEAD_DIM = 64
CHUNK = 64
N_HEADS_SB = 8
N_HEADS_DIFF = 4
ROPE_THETA = 500000.0
ROT_DIM = HEAD_DIM // 4
LEFT_CHUNKS = 8
REL_CLIP = 128
N_EXPERTS = 8
TOP_K = 2
RMS_EPS = 1e-6
ATTN_SCALE = HEAD_DIM ** -0.5

LANES = 128
VMEM_LIMIT_BYTES = 48 * 1024 * 1024

SB_ZERO_COST = 160.0

LOG2E = math.log2(math.e)
FIXED_STABILISER_MAX = 60.0
BOUND_SLACK = 1.02

TM_PROJ = 512
TQ_SB = 256
TQ_DIFF = 1024
TQ_BAND = 256
BAND_WINDOW = TQ_BAND + LEFT_CHUNKS * CHUNK
BAND_UNROLL = 5
TF_DENSE = 1408
TF_EXPERT = 1792
TM_EXPERT = 512
TM_ROUTE = 512
TM_ROWS = 256
ROW_DMA_UNROLL = 8

_NT = (((1,), (1,)), ((), ()))


def _params(*sem):
    return pltpu.CompilerParams(dimension_semantics=sem, vmem_limit_bytes=VMEM_LIMIT_BYTES)


def _rms(x, g):
    return x * lax.rsqrt(jnp.mean(x * x, axis=-1, keepdims=True) + RMS_EPS) * g


def _split_bf16(x):
    hi = x.astype(BF16)
    lo = (x - hi.astype(F32)).astype(BF16)
    return hi, lo


def _lane_sums(p):
    part = p[:, 0:LANES]
    for c in range(LANES, p.shape[1], LANES):
        part = part + p[:, c:c + LANES]
    return part


def _norm_proj_kernel(x_ref, g_ref, w_ref, o_ref, *, n_chunk):
    h = _rms(x_ref[...], g_ref[...]).astype(BF16)
    for c in range(0, o_ref.shape[1], n_chunk):
        o_ref[:, c:c + n_chunk] = jnp.dot(
            h, w_ref[:, c:c + n_chunk], preferred_element_type=F32).astype(o_ref.dtype)


def norm_proj(x, g, w):
    t, d = x.shape
    n = w.shape[1]
    tm = min(TM_PROJ, t)
    return pl.pallas_call(
        functools.partial(_norm_proj_kernel, n_chunk=512),
        grid=(t // tm,),
        in_specs=[pl.BlockSpec((tm, d), lambda i: (i, 0)),
                  pl.BlockSpec((1, d), lambda i: (0, 0)),
                  pl.BlockSpec((d, n), lambda i: (0, 0))],
        out_specs=pl.BlockSpec((tm, n), lambda i: (i, 0)),
        out_shape=jax.ShapeDtypeStruct((t, n), BF16),
        compiler_params=_params("parallel"),
        name="norm_proj",
    )(x, g.reshape(1, d), w)


def _head_mean_sq(x, g_ref):
    hi, lo = _split_bf16(x * x)
    g = g_ref[...]
    return jnp.dot(hi, g, preferred_element_type=F32) + jnp.dot(lo, g, preferred_element_type=F32)


def _qk_prep_kernel(*refs, rope):
    if rope:
        q_ref, k_ref, gq_ref, gk_ref, grp_ref, pos_ref, invf_ref, qo_ref, ko_ref = refs
        ang = pos_ref[...] * invf_ref[...]
        d = lax.broadcasted_iota(jnp.int32, (1, LANES), 1) % HEAD_DIM
        cos = jnp.cos(ang)
        sin = jnp.sin(ang)
        coef_fwd = jnp.where(d < ROT_DIM // 2, -sin, 0.0)
        coef_bwd = jnp.where((d >= ROT_DIM // 2) & (d < ROT_DIM), sin, 0.0)
    else:
        q_ref, k_ref, gq_ref, gk_ref, grp_ref, qo_ref, ko_ref = refs
    width = q_ref.shape[1]
    for src, gain, dst, scale in ((q_ref, gq_ref, qo_ref, ATTN_SCALE * LOG2E), (k_ref, gk_ref, ko_ref, 1.0)):
        for c in range(0, width, LANES):
            x = src[:, c:c + LANES].astype(F32)
            y = x * lax.rsqrt(_head_mean_sq(x, grp_ref) + RMS_EPS) * gain[...]
            if rope:
                half = ROT_DIM // 2
                y = (y * cos + pltpu.roll(y, LANES - half, 1) * coef_fwd
                     + pltpu.roll(y, half, 1) * coef_bwd)
            dst[:, c:c + LANES] = (y * scale).astype(dst.dtype)


def qk_prep(proj, q_col, k_col, width, gq, gk, pos=None):
    t = proj.shape[0]
    tm = min(TM_PROJ, t)
    rope = pos is not None
    grp = np.kron(np.eye(LANES // HEAD_DIM), np.full((HEAD_DIM, HEAD_DIM), 1.0 / HEAD_DIM))
    vec = lambda g: jnp.tile(g.astype(F32), LANES // HEAD_DIM).reshape(1, LANES)
    args = [proj, proj, vec(gq), vec(gk), jnp.asarray(grp, BF16)]
    const = lambda shape: pl.BlockSpec(shape, lambda i: (0, 0))
    in_specs = [pl.BlockSpec((tm, width), lambda i: (i, q_col)),
                pl.BlockSpec((tm, width), lambda i: (i, k_col)),
                const((1, LANES)), const((1, LANES)), const((LANES, LANES))]
    if rope:
        inv_freq = ROPE_THETA ** (-jnp.arange(0, ROT_DIM, 2, dtype=F32) / ROT_DIM)
        per_head = jnp.concatenate([inv_freq, inv_freq, jnp.zeros((HEAD_DIM - ROT_DIM,), F32)])
        args += [pos.astype(F32).reshape(t, 1), jnp.tile(per_head, LANES // HEAD_DIM).reshape(1, LANES)]
        in_specs += [pl.BlockSpec((tm, 1), lambda i: (i, 0)), const((1, LANES))]
    out = jax.ShapeDtypeStruct((t, width), BF16)
    return pl.pallas_call(
        functools.partial(_qk_prep_kernel, rope=rope),
        grid=(t // tm,),
        in_specs=in_specs,
        out_specs=[pl.BlockSpec((tm, width), lambda i: (i, 0))] * 2,
        out_shape=[out, out],
        compiler_params=_params("parallel"),
        name="qk_prep_rope" if rope else "qk_prep",
    )(*args)


def _sb_kernel(q_ref, k_ref, v_ref, o_ref, tri_ref, cost_ref, acc_ref, *, tq):
    i = pl.program_id(2)
    n_heads = LANES // HEAD_DIM
    lane = lax.broadcasted_iota(jnp.int32, (1, LANES), 1)
    row = lax.broadcasted_iota(jnp.int32, (tq, tq), 0)
    col = lax.broadcasted_iota(jnp.int32, (tq, tq), 1)
    tri_ref[...] = jnp.where(row > col, 1.0, 0.0).astype(BF16)
    q = q_ref[...]
    q_all = jnp.concatenate(
        [jnp.where((lane // HEAD_DIM) == h, q, jnp.zeros_like(q)) for h in range(n_heads)], axis=0)
    cost_ref[...] = jnp.zeros_like(cost_ref)
    acc_ref[...] = jnp.zeros_like(acc_ref)

    def block(j, diag):
        start = pl.multiple_of(j * tq, tq)
        k = k_ref[pl.ds(start, tq), :]
        v = v_ref[pl.ds(start, tq), :]
        tri = tri_ref[...]
        z = lax.dot_general(q_all, k, _NT, preferred_element_type=F32)
        tail = jnp.log2(1.0 + jnp.exp2(-jnp.abs(z)))
        cost_stay = jnp.maximum(z, 0.0) + tail
        if diag:
            past = jnp.concatenate([col < row] * n_heads, axis=0)
            cost_stay = jnp.where(past, cost_stay, 0.0)
        hi, lo = _split_bf16(cost_stay)
        between = (jnp.dot(hi, tri, preferred_element_type=F32)
                   + jnp.dot(lo, tri, preferred_element_type=F32))
        w = jnp.exp2(jnp.minimum(z, 0.0) - tail - between - cost_ref[...])
        if diag:
            w = jnp.where(past, w, 0.0)
        acc_ref[...] += jnp.dot(w.astype(BF16), v, preferred_element_type=F32)
        cost_ref[...] += jnp.sum(_lane_sums(cost_stay), axis=-1, keepdims=True)

    @pl.when(i == 0)
    def _():
        block(i, True)

    @pl.when(i > 0)
    def _():
        block(i, True)
        block(i - 1, False)

    def any_live():
        return (jnp.min(cost_ref[...]) < SB_ZERO_COST).astype(jnp.int32)

    def cond(state):
        j, live = state
        return jnp.logical_and(j >= 0, live > 0)

    def body(state):
        j, _ = state
        block(j, False)
        return j - 1, any_live()

    lax.while_loop(cond, body, (i - 2, any_live()))
    out = jnp.zeros((tq, LANES), F32)
    for h in range(n_heads):
        out = out + jnp.where((lane // HEAD_DIM) == h, acc_ref[h * tq:(h + 1) * tq, :], 0.0)
    o_ref[...] = out.astype(o_ref.dtype)


def sb_attention(proj, batch, seq):
    t = proj.shape[0]
    tq = min(TQ_SB, seq)
    nq = seq // tq
    n_pair = N_HEADS_SB * HEAD_DIM // LANES
    return pl.pallas_call(
        functools.partial(_sb_kernel, tq=tq),
        grid=(batch, n_pair, nq),
        in_specs=[pl.BlockSpec((tq, LANES), lambda b, p, i: (b * nq + i, p)),
                  pl.BlockSpec((seq, LANES), lambda b, p, i: (b, n_pair + p)),
                  pl.BlockSpec((seq, LANES), lambda b, p, i: (b, 2 * n_pair + p))],
        out_specs=pl.BlockSpec((tq, LANES), lambda b, p, i: (b * nq + i, p)),
        out_shape=jax.ShapeDtypeStruct((t, n_pair * LANES), BF16),
        scratch_shapes=[pltpu.VMEM((tq, tq), BF16), pltpu.VMEM((LANES // HEAD_DIM * tq, 1), F32),
                        pltpu.VMEM((LANES // HEAD_DIM * tq, LANES), F32)],
        compiler_params=_params("parallel", "parallel", "arbitrary"),
        name="sb_attention",
    )(proj, proj, proj)


def _diff_kernel(bound_ref, q_ref, k_ref, v_ref, lam_ref, subln_ref, o_ref, m_ref, l_ref, acc_ref, *,
                 tq, lam_init):
    i = pl.program_id(2)
    bound = bound_ref[0, 0]
    lane = lax.broadcasted_iota(jnp.int32, (1, LANES), 1)
    q = q_ref[...]
    q_maps = [jnp.where((lane // HEAD_DIM) == m, q, jnp.zeros_like(q)) for m in range(2)]
    l_ref[...] = jnp.zeros_like(l_ref)
    acc_ref[...] = jnp.zeros_like(acc_ref)

    def scores(m, k, diag, r0=0, c0=0):
        nr, nk = tq - r0, k.shape[0]
        s = lax.dot_general(q_maps[m][r0:], k, _NT, preferred_element_type=F32)
        if diag:
            row = r0 + lax.broadcasted_iota(jnp.int32, (nr, nk), 0)
            col = c0 + lax.broadcasted_iota(jnp.int32, (nr, nk), 1)
            s = jnp.where((col // CHUNK) <= (row // CHUNK), s, -jnp.inf)
        return s

    def fixed_part(j, diag, r0, c0, nk):
        start = pl.multiple_of(j * tq + c0, nk)
        k = k_ref[pl.ds(start, nk), :]
        v = v_ref[pl.ds(start, nk), :]
        for m in range(2):
            p = jnp.exp2(scores(m, k, diag, r0, c0) - bound)
            l_ref[m, r0:] += _lane_sums(p)
            acc_ref[m, r0:] += jnp.dot(p.astype(BF16), v, preferred_element_type=F32)

    def fixed_step(j, diag):
        if diag:
            fixed_part(j, True, 0, 0, tq // 2)
            fixed_part(j, True, tq // 2, tq // 2, tq // 2)
        else:
            fixed_part(j, False, 0, 0, tq)

    def online_step(j, diag):
        start = pl.multiple_of(j * tq, tq)
        k = k_ref[pl.ds(start, tq), :]
        v = v_ref[pl.ds(start, tq), :]
        for m in range(2):
            s = scores(m, k, diag)
            m_old = m_ref[m]
            m_new = jnp.maximum(m_old, jnp.max(s, axis=-1, keepdims=True))
            p = jnp.exp2(s - m_new)
            alpha = jnp.exp2(m_old - m_new)
            l_ref[m] = alpha * l_ref[m] + _lane_sums(p)
            acc_ref[m] = alpha * acc_ref[m] + jnp.dot(p.astype(BF16), v, preferred_element_type=F32)
            m_ref[m] = m_new

    def sweep(step):
        def body(j, c):
            step(j, False)
            return c
        lax.fori_loop(0, i, body, 0)
        step(i, True)

    @pl.when(bound <= FIXED_STABILISER_MAX)
    def _():
        sweep(fixed_step)

    @pl.when(jnp.logical_not(bound <= FIXED_STABILISER_MAX))
    def _():
        m_ref[...] = jnp.full_like(m_ref, -jnp.inf)
        sweep(online_step)

    lam_vecs = lam_ref[...]
    lam = (jnp.exp(jnp.sum(lam_vecs[0:1] * lam_vecs[1:2], axis=-1, keepdims=True))
           - jnp.exp(jnp.sum(lam_vecs[2:3] * lam_vecs[3:4], axis=-1, keepdims=True)) + lam_init)
    norm = [jnp.sum(l_ref[m], axis=-1, keepdims=True) for m in range(2)]
    o = acc_ref[0] / norm[0] - lam * (acc_ref[1] / norm[1])
    o_ref[...] = (_rms(o, subln_ref[...]) * (1.0 - lam_init)).astype(o_ref.dtype)


def diff_attention(qd, kd, proj, score_bound, lam_vecs, subln, lam_init, batch, seq):
    t = qd.shape[0]
    tq = min(TQ_DIFF, seq)
    nq = seq // tq
    v_col = (3 * N_HEADS_SB * HEAD_DIM + 2 * N_HEADS_DIFF * 2 * HEAD_DIM) // LANES
    return pl.pallas_call(
        functools.partial(_diff_kernel, tq=tq, lam_init=lam_init),
        grid=(batch, N_HEADS_DIFF, nq),
        in_specs=[pl.BlockSpec(memory_space=pltpu.SMEM),
                  pl.BlockSpec((tq, LANES), lambda b, h, i: (b * nq + i, h)),
                  pl.BlockSpec((seq, LANES), lambda b, h, i: (b, h)),
                  pl.BlockSpec((seq, LANES), lambda b, h, i: (b, v_col + h)),
                  pl.BlockSpec((4, HEAD_DIM), lambda b, h, i: (0, 0)),
                  pl.BlockSpec((1, LANES), lambda b, h, i: (0, 0))],
        out_specs=pl.BlockSpec((tq, LANES), lambda b, h, i: (b * nq + i, h)),
        out_shape=jax.ShapeDtypeStruct((t, N_HEADS_DIFF * LANES), BF16),
        scratch_shapes=[pltpu.VMEM((2, tq, 1), F32), pltpu.VMEM((2, tq, LANES), F32),
                        pltpu.VMEM((2, tq, LANES), F32)],
        compiler_params=_params("parallel", "parallel", "arbitrary"),
        name="diff_attention",
    )(score_bound.reshape(1, 1), qd, kd, proj, lam_vecs, subln.reshape(1, LANES))


def _band_kernel(bound_ref, q_ref, k_ref, v_ref, bias_ref, o_ref, *, seq):
    bound = bound_ref[0, 0]
    lane = lax.broadcasted_iota(jnp.int32, (1, LANES), 1)
    left = BAND_WINDOW - TQ_BAND

    def tile(t0, k_start, n_keys, fixed):
        q = q_ref[pl.ds(t0, TQ_BAND), :]
        k = k_ref[pl.ds(k_start, n_keys), :]
        v = v_ref[pl.ds(k_start, n_keys), :]
        out = jnp.zeros((TQ_BAND, LANES), F32)
        for half in range(LANES // HEAD_DIM):
            in_head = (lane // HEAD_DIM) == half
            qh = jnp.where(in_head, q, jnp.zeros_like(q))
            s = lax.dot_general(qh, k, _NT, preferred_element_type=F32)
            s = s + bias_ref[half, :, BAND_WINDOW - n_keys:]
            p = jnp.exp2(s - (bound if fixed else jnp.max(s, axis=-1, keepdims=True)))
            o = jnp.dot(p.astype(BF16), v, preferred_element_type=F32)
            norm = jnp.sum(_lane_sums(p), axis=-1, keepdims=True)
            out = out + jnp.where(in_head, o / norm, 0.0)
        o_ref[pl.ds(t0, TQ_BAND), :] = out.astype(o_ref.dtype)

    def sweep(fixed):
        n_tiles = seq // TQ_BAND
        n_edge = min(left // TQ_BAND, n_tiles)
        for i in range(n_edge):
            tile(i * TQ_BAND, 0, (i + 1) * TQ_BAND, fixed)

        def body(i, c):
            t0 = pl.multiple_of(i * TQ_BAND, TQ_BAND)
            tile(t0, pl.multiple_of(t0 - left, TQ_BAND), BAND_WINDOW, fixed)
            return c

        lax.fori_loop(n_edge, n_tiles, body, 0, unroll=BAND_UNROLL)

    @pl.when(bound <= FIXED_STABILISER_MAX)
    def _():
        sweep(True)

    @pl.when(jnp.logical_not(bound <= FIXED_STABILISER_MAX))
    def _():
        sweep(False)


def band_bias_table(rel_table):
    left = BAND_WINDOW - TQ_BAND
    period = TQ_BAND + BAND_WINDOW
    k = np.arange(period)
    k = np.where(k < BAND_WINDOW, k, k - period)
    by_offset = rel_table.astype(F32)[:, np.clip(left - k, -REL_CLIP, REL_CLIP) + REL_CLIP]
    skew = jnp.tile(by_offset, (1, TQ_BAND))[:, :TQ_BAND * (period - 1)]
    bias = skew.reshape(-1, TQ_BAND, period - 1)[:, :, :BAND_WINDOW]
    q_chunk = np.arange(TQ_BAND)[:, None] // CHUNK
    k_chunk = np.arange(BAND_WINDOW)[None, :] // CHUNK
    in_band = (k_chunk >= q_chunk) & (k_chunk <= q_chunk + LEFT_CHUNKS)
    return jnp.where(jnp.asarray(in_band)[None], bias * LOG2E, -jnp.inf)


def score_bound(gq, gk, bias_max=0.0):
    qk = HEAD_DIM * ATTN_SCALE * jnp.max(jnp.abs(gq)) * jnp.max(jnp.abs(gk))
    return ((qk + bias_max) * (LOG2E * BOUND_SLACK)).astype(F32)


def band_attention(qn, kn, proj, bias, score_bound, batch, seq):
    t, width = qn.shape
    n_pair = width // LANES
    seq_block = lambda col0: pl.BlockSpec((seq, LANES), lambda b, p: (b, col0 + p))
    return pl.pallas_call(
        functools.partial(_band_kernel, seq=seq),
        grid=(batch, n_pair),
        in_specs=[pl.BlockSpec(memory_space=pltpu.SMEM),
                  seq_block(0), seq_block(0), seq_block(2 * n_pair),
                  pl.BlockSpec((LANES // HEAD_DIM, TQ_BAND, BAND_WINDOW), lambda b, p: (p, 0, 0))],
        out_specs=seq_block(0),
        out_shape=jax.ShapeDtypeStruct((t, width), BF16),
        compiler_params=_params("parallel", "parallel"),
        name="band_attention",
    )(score_bound.reshape(1, 1), qn, kn, proj, bias)


def _out_proj_kernel(*refs):
    x_ref, *part_refs, w_ref, o_ref = refs
    acc = x_ref[...]
    row = 0
    for part in part_refs:
        d = part.shape[1]
        acc = acc + jnp.dot(part[...], w_ref[row:row + d, :], preferred_element_type=F32)
        row += d
    o_ref[...] = acc


def out_proj_residual(x, parts, w):
    t, d = x.shape
    tm = min(TM_PROJ, t)
    return pl.pallas_call(
        _out_proj_kernel,
        grid=(t // tm,),
        in_specs=([pl.BlockSpec((tm, d), lambda i: (i, 0))]
                  + [pl.BlockSpec((tm, p.shape[1]), lambda i: (i, 0)) for p in parts]
                  + [pl.BlockSpec(w.shape, lambda i: (0, 0))]),
        out_specs=pl.BlockSpec((tm, d), lambda i: (i, 0)),
        out_shape=jax.ShapeDtypeStruct((t, d), F32),
        compiler_params=_params("parallel"),
        name="out_proj_residual",
    )(x, *parts, w)


def _swiglu_kernel(x_ref, g_ref, wg_ref, wu_ref, wd_ref, o_ref, h_ref, acc_ref):
    j = pl.program_id(1)

    @pl.when(j == 0)
    def _():
        x = x_ref[...]
        h_ref[...] = _rms(x, g_ref[...]).astype(BF16)
        acc_ref[...] = x

    h = h_ref[...]
    gate = jnp.dot(h, wg_ref[...], preferred_element_type=F32)
    up = jnp.dot(h, wu_ref[...], preferred_element_type=F32)
    act = (gate * jax.nn.sigmoid(gate) * up).astype(BF16)
    acc_ref[...] += jnp.dot(act, wd_ref[...], preferred_element_type=F32)

    @pl.when(j == pl.num_programs(1) - 1)
    def _():
        o_ref[...] = acc_ref[...]


def swiglu_residual(x, g, wg, wu, wd):
    t, d = x.shape
    f = wg.shape[1]
    tm = min(TM_PROJ, t)
    tf = TF_DENSE if f % TF_DENSE == 0 else f
    return pl.pallas_call(
        _swiglu_kernel,
        grid=(t // tm, f // tf),
        in_specs=[pl.BlockSpec((tm, d), lambda i, j: (i, 0)),
                  pl.BlockSpec((1, d), lambda i, j: (0, 0)),
                  pl.BlockSpec((d, tf), lambda i, j: (0, j)),
                  pl.BlockSpec((d, tf), lambda i, j: (0, j)),
                  pl.BlockSpec((tf, d), lambda i, j: (j, 0))],
        out_specs=pl.BlockSpec((tm, d), lambda i, j: (i, 0)),
        out_shape=jax.ShapeDtypeStruct((t, d), F32),
        scratch_shapes=[pltpu.VMEM((tm, d), BF16), pltpu.VMEM((tm, d), F32)],
        compiler_params=_params("parallel", "arbitrary"),
        name="swiglu_residual",
    )(x, g.reshape(1, d), wg, wu, wd)


def _router_kernel(x_ref, g_ref, wr_ref, h_ref, pick_ref, gate_ref, count_ref, carry_ref, *, tm):
    i = pl.program_id(0)

    @pl.when(i == 0)
    def _():
        carry_ref[...] = jnp.zeros_like(carry_ref)

    h = _rms(x_ref[...], g_ref[...])
    h_ref[...] = h
    logits = lax.dot_general(wr_ref[...], h, _NT, preferred_element_type=F32,
                             precision=lax.Precision.HIGHEST)
    e_idx = lax.broadcasted_iota(jnp.int32, (N_EXPERTS, tm), 0)

    def top1(vals):
        best = jnp.max(vals, axis=0, keepdims=True)
        arg = jnp.min(jnp.where(vals == best, e_idx, N_EXPERTS), axis=0, keepdims=True)
        return best, arg

    v1, i1 = top1(logits)
    v2, i2 = top1(jnp.where(e_idx == i1, -jnp.inf, logits))
    ratio = jnp.exp(v2 - v1)
    g1 = 1.0 / (1.0 + ratio)
    g2 = ratio / (1.0 + ratio)

    picked = jnp.where((e_idx == i1) | (e_idx == i2), 1.0, 0.0)
    a = lax.broadcasted_iota(jnp.int32, (tm, tm), 0)
    b = lax.broadcasted_iota(jnp.int32, (tm, tm), 1)
    before = jnp.where(a < b, 1.0, 0.0).astype(BF16)
    rank = carry_ref[...] + jnp.dot(picked.astype(BF16), before, preferred_element_type=F32)
    rank = rank.astype(jnp.int32)
    r1 = jnp.sum(jnp.where(e_idx == i1, rank, 0), axis=0, keepdims=True)
    r2 = jnp.sum(jnp.where(e_idx == i2, rank, 0), axis=0, keepdims=True)
    carry_ref[...] += jnp.sum(picked, axis=1, keepdims=True)

    pick_ref[...] = jnp.zeros_like(pick_ref)
    gate_ref[...] = jnp.zeros_like(gate_ref)
    for r, val in enumerate((i1, i2, r1, r2)):
        pick_ref[r:r + 1, :] = val
    for r, val in enumerate((g1, g2)):
        gate_ref[r:r + 1, :] = val
    count_ref[...] = jnp.broadcast_to(carry_ref[...], count_ref.shape).astype(jnp.int32)


def router(x, g, w_router):
    t, d = x.shape
    tm = min(TM_ROUTE, t)
    return pl.pallas_call(
        functools.partial(_router_kernel, tm=tm),
        grid=(t // tm,),
        in_specs=[pl.BlockSpec((tm, d), lambda i: (i, 0)),
                  pl.BlockSpec((1, d), lambda i: (0, 0)),
                  pl.BlockSpec((N_EXPERTS, d), lambda i: (0, 0))],
        out_specs=[pl.BlockSpec((tm, d), lambda i: (i, 0)),
                   pl.BlockSpec((8, tm), lambda i: (0, i)),
                   pl.BlockSpec((8, tm), lambda i: (0, i)),
                   pl.BlockSpec((N_EXPERTS, LANES), lambda i: (0, 0))],
        out_shape=[jax.ShapeDtypeStruct((t, d), F32),
                   jax.ShapeDtypeStruct((8, t), jnp.int32),
                   jax.ShapeDtypeStruct((8, t), F32),
                   jax.ShapeDtypeStruct((N_EXPERTS, LANES), jnp.int32)],
        scratch_shapes=[pltpu.VMEM((N_EXPERTS, 1), F32)],
        compiler_params=_params("arbitrary"),
        name="router",
    )(x, g.reshape(1, d), w_router.T)


def _row_copy(src_ref, src_row, dst_ref, dst_row, sem):
    return pltpu.make_async_copy(src_ref.at[pl.ds(src_row, 1)], dst_ref.at[pl.ds(dst_row, 1)], sem)


def _dispatch_kernel(dest_ref, h_ref, init_ref, xs_ref, sem, *, tm):
    del init_ref

    def issue(t, c):
        for k in range(TOP_K):
            _row_copy(h_ref, t, xs_ref, dest_ref[0, k, t], sem).start()
        return c

    lax.fori_loop(0, tm, issue, 0, unroll=ROW_DMA_UNROLL)
    for k in range(TOP_K):
        pltpu.make_async_copy(h_ref, xs_ref.at[pl.ds(0, tm)], sem).wait()


def dispatch_rows(h, dest, n_rows):
    t, d = h.shape
    tm = min(TM_ROWS, t)
    return pl.pallas_call(
        functools.partial(_dispatch_kernel, tm=tm),
        grid=(t // tm,),
        in_specs=[pl.BlockSpec((1, TOP_K, tm), lambda i: (i, 0, 0), memory_space=pltpu.SMEM),
                  pl.BlockSpec((tm, d), lambda i: (i, 0)),
                  pl.BlockSpec(memory_space=pl.ANY)],
        out_specs=pl.BlockSpec(memory_space=pl.ANY),
        out_shape=jax.ShapeDtypeStruct((n_rows, d), h.dtype),
        scratch_shapes=[pltpu.SemaphoreType.DMA(())],
        input_output_aliases={2: 0},
        compiler_params=_params("arbitrary"),
        name="dispatch_rows",
    )(dest, h, jnp.zeros((n_rows, d), h.dtype))


def _combine_kernel(dest_ref, x_ref, gate_ref, ys_ref, o_ref, buf_ref, sem, *, tm):
    def issue(t, c):
        for k in range(TOP_K):
            _row_copy(ys_ref, dest_ref[0, k, t], buf_ref.at[k], t, sem).start()
        return c

    lax.fori_loop(0, tm, issue, 0, unroll=ROW_DMA_UNROLL)
    for k in range(TOP_K):
        pltpu.make_async_copy(ys_ref.at[pl.ds(0, tm)], buf_ref.at[k], sem).wait()
    gate = gate_ref[...]
    o_ref[...] = x_ref[...] + gate[:, 0:1] * buf_ref[0] + gate[:, 1:2] * buf_ref[1]


def combine_rows(x, gates, ys, dest):
    t, d = x.shape
    tm = min(TM_ROWS, t)
    return pl.pallas_call(
        functools.partial(_combine_kernel, tm=tm),
        grid=(t // tm,),
        in_specs=[pl.BlockSpec((1, TOP_K, tm), lambda i: (i, 0, 0), memory_space=pltpu.SMEM),
                  pl.BlockSpec((tm, d), lambda i: (i, 0)),
                  pl.BlockSpec((tm, TOP_K), lambda i: (i, 0)),
                  pl.BlockSpec(memory_space=pl.ANY)],
        out_specs=pl.BlockSpec((tm, d), lambda i: (i, 0)),
        out_shape=jax.ShapeDtypeStruct((t, d), F32),
        scratch_shapes=[pltpu.VMEM((TOP_K, tm, d), F32), pltpu.SemaphoreType.DMA(())],
        compiler_params=_params("arbitrary"),
        name="combine_rows",
    )(dest, x, gates, ys)


def _expert_ffn_kernel(expert_ref, live_ref, xs_ref, wg_ref, wu_ref, wd_ref, o_ref, h_ref, acc_ref):
    del expert_ref
    i, j = pl.program_id(0), pl.program_id(1)
    live = live_ref[i] > 0

    @pl.when(j == 0)
    def _():
        h_ref[...] = xs_ref[...].astype(BF16)
        acc_ref[...] = jnp.zeros_like(acc_ref)

    @pl.when(live)
    def _():
        h = h_ref[...]
        gate = jnp.dot(h, wg_ref[0], preferred_element_type=F32)
        up = jnp.dot(h, wu_ref[0], preferred_element_type=F32)
        act = (gate * jax.nn.sigmoid(gate) * up).astype(BF16)
        acc_ref[...] += jnp.dot(act, wd_ref[0], preferred_element_type=F32)

    @pl.when(j == pl.num_programs(1) - 1)
    def _():
        o_ref[...] = acc_ref[...]


def expert_ffn(xs, tile_expert, tile_live, wg, wu, wd):
    r, d = xs.shape
    f = wg.shape[2]
    tm = TM_EXPERT
    tf = TF_EXPERT if f % TF_EXPERT == 0 else f
    grid_spec = pltpu.PrefetchScalarGridSpec(
        num_scalar_prefetch=2,
        grid=(r // tm, f // tf),
        in_specs=[pl.BlockSpec((tm, d), lambda i, j, e, v: (i, 0)),
                  pl.BlockSpec((1, d, tf), lambda i, j, e, v: (e[i], 0, j)),
                  pl.BlockSpec((1, d, tf), lambda i, j, e, v: (e[i], 0, j)),
                  pl.BlockSpec((1, tf, d), lambda i, j, e, v: (e[i], j, 0))],
        out_specs=pl.BlockSpec((tm, d), lambda i, j, e, v: (i, 0)),
        scratch_shapes=[pltpu.VMEM((tm, d), BF16), pltpu.VMEM((tm, d), F32)])
    return pl.pallas_call(
        _expert_ffn_kernel,
        grid_spec=grid_spec,
        out_shape=jax.ShapeDtypeStruct((r, d), F32),
        compiler_params=_params("parallel", "arbitrary"),
        name="expert_ffn",
    )(tile_expert, tile_live, xs, wg, wu, wd)


def moe_residual(x, g, w_router, wg, wu, wd):
    t, d = x.shape
    tm = TM_EXPERT
    n_rows = TOP_K * t + N_EXPERTS * tm
    n_tiles = n_rows // tm
    h, picks, gates, counts = router(x, g, w_router)

    counts = counts[:, 0]
    padded = (counts + tm - 1) // tm * tm
    ends = jnp.cumsum(padded)
    starts = ends - padded
    experts, ranks = picks[0:TOP_K], picks[TOP_K:2 * TOP_K]
    dest = jnp.sum(jnp.where(experts[None] == jnp.arange(N_EXPERTS)[:, None, None],
                             starts[:, None, None], 0), axis=0) + ranks
    tile_start = jnp.arange(n_tiles, dtype=jnp.int32) * tm
    tile_expert = jnp.minimum(jnp.sum(tile_start[:, None] >= ends[None, :], axis=1),
                              N_EXPERTS - 1).astype(jnp.int32)
    tile_live = (tile_start < ends[-1]).astype(jnp.int32)

    tr = min(TM_ROWS, t)
    dest_tiles = dest.reshape(TOP_K, t // tr, tr).transpose(1, 0, 2)
    xs = dispatch_rows(h, dest_tiles, n_rows)
    ys = expert_ffn(xs, tile_expert, tile_live, wg, wu, wd)
    return combine_rows(x, gates[0:TOP_K].T, ys, dest_tiles)


def kernel(x, positions, ev_attn_norm, ev_w_in, ev_q_norm, ev_k_norm, ev_lambda_q1, ev_lambda_k1, ev_lambda_q2, ev_lambda_k2, ev_subln, ev_w_out, ev_ffn_norm, ev_w_gate, ev_w_up, ev_w_down, od_attn_norm, od_w_qkv, od_q_norm, od_k_norm, od_rel_bias, od_w_out, od_ffn_norm, od_router, od_we_gate, od_we_up, od_we_down):
    batch, seq, d_model = x.shape
    depth = ev_attn_norm.shape[0] + od_attn_norm.shape[0]
    bf = lambda w: w.astype(BF16)
    x = x.reshape(batch * seq, d_model)
    d_sb = N_HEADS_SB * HEAD_DIM
    d_diff = N_HEADS_DIFF * 2 * HEAD_DIM
    for layer in range(depth):
        i = layer // 2
        if layer % 2 == 0:
            col_scale = jnp.where(jnp.arange(ev_w_in.shape[2]) < d_sb, ATTN_SCALE * LOG2E, 1.0)
            proj = norm_proj(x, ev_attn_norm[i], bf(ev_w_in[i] * col_scale))
            o_sb = sb_attention(proj, batch, seq)
            qd, kd = qk_prep(proj, 3 * d_sb // d_diff, 3 * d_sb // d_diff + 1, d_diff,
                             ev_q_norm[i], ev_k_norm[i], positions)
            lam_init = 0.8 - 0.6 * math.exp(-0.3 * layer)
            lam_vecs = jnp.stack([ev_lambda_q1[i], ev_lambda_k1[i], ev_lambda_q2[i], ev_lambda_k2[i]])
            o_df = diff_attention(qd, kd, proj, score_bound(ev_q_norm[i], ev_k_norm[i]),
                                  lam_vecs.astype(F32), ev_subln[i], lam_init, batch, seq)
            x = out_proj_residual(x, [o_sb, o_df], bf(ev_w_out[i]))
            x = swiglu_residual(x, ev_ffn_norm[i], bf(ev_w_gate[i]), bf(ev_w_up[i]), bf(ev_w_down[i]))
        else:
            proj = norm_proj(x, od_attn_norm[i], bf(od_w_qkv[i]))
            qn, kn = qk_prep(proj, 0, 1, d_model, od_q_norm[i], od_k_norm[i])
            bound = score_bound(od_q_norm[i], od_k_norm[i], jnp.max(jnp.abs(od_rel_bias[i])))
            o_band = band_attention(qn, kn, proj, band_bias_table(od_rel_bias[i]), bound, batch, seq)
            x = out_proj_residual(x, [o_band], bf(od_w_out[i]))
            x = moe_residual(x, od_ffn_norm[i], od_router[i], bf(od_we_gate[i]), bf(od_we_up[i]),
                             bf(od_we_down[i]))
    return x.reshape(batch, seq, d_model)
```

```python
import functools
import math

import numpy as np
import jax
import jax.numpy as jnp
from jax import lax
from jax.experimental import pallas as pl
from jax.experimental.pallas import tpu as pltpu

F32 = jnp.float32
BF16 = jnp.bfloat16

HEAD_DIM = 64
CHUNK = 64
N_HEADS_SB = 8
N_HEADS_DIFF = 4
ROPE_THETA = 500000.0
ROT_DIM = HEAD_DIM // 4
LEFT_CHUNKS = 8
REL_CLIP = 128
N_EXPERTS = 8
TOP_K = 2
RMS_EPS = 1e-6
ATTN_SCALE = HEAD_DIM ** -0.5

LANES = 128
VMEM_LIMIT_BYTES = 48 * 1024 * 1024

SB_ZERO_COST = 160.0

LOG2E = math.log2(math.e)
FIXED_STABILISER_MAX = 60.0
BOUND_SLACK = 1.02

TM_PROJ = 512
TQ_SB = 256
TQ_DIFF = 1024
TQ_BAND = 256
BAND_WINDOW = TQ_BAND + LEFT_CHUNKS * CHUNK
BAND_UNROLL = 5
TF_DENSE = 1408
TF_EXPERT = 1792
TM_EXPERT = 512
TM_ROUTE = 512
TM_ROWS = 256
ROW_DMA_UNROLL = 8

_NT = (((1,), (1,)), ((), ()))


def _params(*sem):
    return pltpu.CompilerParams(dimension_semantics=sem, vmem_limit_bytes=VMEM_LIMIT_BYTES)


def _rms(x, g):
    return x * lax.rsqrt(jnp.mean(x * x, axis=-1, keepdims=True) + RMS_EPS) * g


def _split_bf16(x):
    hi = x.astype(BF16)
    lo = (x - hi.astype(F32)).astype(BF16)
    return hi, lo


def _lane_sums(p):
    part = p[:, 0:LANES]
    for c in range(LANES, p.shape[1], LANES):
        part = part + p[:, c:c + LANES]
    return part


def _norm_proj_kernel(x_ref, g_ref, w_ref, o_ref, *, n_chunk):
    h = _rms(x_ref[...], g_ref[...]).astype(BF16)
    for c in range(0, o_ref.shape[1], n_chunk):
        o_ref[:, c:c + n_chunk] = jnp.dot(
            h, w_ref[:, c:c + n_chunk], preferred_element_type=F32).astype(o_ref.dtype)


def norm_proj(x, g, w):
    t, d = x.shape
    n = w.shape[1]
    tm = min(TM_PROJ, t)
    return pl.pallas_call(
        functools.partial(_norm_proj_kernel, n_chunk=512),
        grid=(t // tm,),
        in_specs=[pl.BlockSpec((tm, d), lambda i: (i, 0)),
                  pl.BlockSpec((1, d), lambda i: (0, 0)),
                  pl.BlockSpec((d, n), lambda i: (0, 0))],
        out_specs=pl.BlockSpec((tm, n), lambda i: (i, 0)),
        out_shape=jax.ShapeDtypeStruct((t, n), BF16),
        compiler_params=_params("parallel"),
        name="norm_proj",
    )(x, g.reshape(1, d), w)


def _head_mean_sq(x, g_ref):
    hi, lo = _split_bf16(x * x)
    g = g_ref[...]
    return jnp.dot(hi, g, preferred_element_type=F32) + jnp.dot(lo, g, preferred_element_type=F32)


def _qk_prep_kernel(*refs, rope):
    if rope:
        q_ref, k_ref, gq_ref, gk_ref, grp_ref, pos_ref, invf_ref, qo_ref, ko_ref = refs
        ang = pos_ref[...] * invf_ref[...]
        d = lax.broadcasted_iota(jnp.int32, (1, LANES), 1) % HEAD_DIM
        cos = jnp.cos(ang)
        sin = jnp.sin(ang)
        coef_fwd = jnp.where(d < ROT_DIM // 2, -sin, 0.0)
        coef_bwd = jnp.where((d >= ROT_DIM // 2) & (d < ROT_DIM), sin, 0.0)
    else:
        q_ref, k_ref, gq_ref, gk_ref, grp_ref, qo_ref, ko_ref = refs
    width = q_ref.shape[1]
    for src, gain, dst, scale in ((q_ref, gq_ref, qo_ref, ATTN_SCALE * LOG2E), (k_ref, gk_ref, ko_ref, 1.0)):
        for c in range(0, width, LANES):
            x = src[:, c:c + LANES].astype(F32)
            y = x * lax.rsqrt(_head_mean_sq(x, grp_ref) + RMS_EPS) * gain[...]
            if rope:
                half = ROT_DIM // 2
                y = (y * cos + pltpu.roll(y, LANES - half, 1) * coef_fwd
                     + pltpu.roll(y, half, 1) * coef_bwd)
            dst[:, c:c + LANES] = (y * scale).astype(dst.dtype)


def qk_prep(proj, q_col, k_col, width, gq, gk, pos=None):
    t = proj.shape[0]
    tm = min(TM_PROJ, t)
    rope = pos is not None
    grp = np.kron(np.eye(LANES // HEAD_DIM), np.full((HEAD_DIM, HEAD_DIM), 1.0 / HEAD_DIM))
    vec = lambda g: jnp.tile(g.astype(F32), LANES // HEAD_DIM).reshape(1, LANES)
    args = [proj, proj, vec(gq), vec(gk), jnp.asarray(grp, BF16)]
    const = lambda shape: pl.BlockSpec(shape, lambda i: (0, 0))
    in_specs = [pl.BlockSpec((tm, width), lambda i: (i, q_col)),
                pl.BlockSpec((tm, width), lambda i: (i, k_col)),
                const((1, LANES)), const((1, LANES)), const((LANES, LANES))]
    if rope:
        inv_freq = ROPE_THETA ** (-jnp.arange(0, ROT_DIM, 2, dtype=F32) / ROT_DIM)
        per_head = jnp.concatenate([inv_freq, inv_freq, jnp.zeros((HEAD_DIM - ROT_DIM,), F32)])
        args += [pos.astype(F32).reshape(t, 1), jnp.tile(per_head, LANES // HEAD_DIM).reshape(1, LANES)]
        in_specs += [pl.BlockSpec((tm, 1), lambda i: (i, 0)), const((1, LANES))]
    out = jax.ShapeDtypeStruct((t, width), BF16)
    return pl.pallas_call(
        functools.partial(_qk_prep_kernel, rope=rope),
        grid=(t // tm,),
        in_specs=in_specs,
        out_specs=[pl.BlockSpec((tm, width), lambda i: (i, 0))] * 2,
        out_shape=[out, out],
        compiler_params=_params("parallel"),
        name="qk_prep_rope" if rope else "qk_prep",
    )(*args)


def _sb_kernel(q_ref, k_ref, v_ref, o_ref, tri_ref, cost_ref, acc_ref, *, tq):
    i = pl.program_id(2)
    n_heads = LANES // HEAD_DIM
    lane = lax.broadcasted_iota(jnp.int32, (1, LANES), 1)
    row = lax.broadcasted_iota(jnp.int32, (tq, tq), 0)
    col = lax.broadcasted_iota(jnp.int32, (tq, tq), 1)
    tri_ref[...] = jnp.where(row > col, 1.0, 0.0).astype(BF16)
    q = q_ref[...]
    q_all = jnp.concatenate(
        [jnp.where((lane // HEAD_DIM) == h, q, jnp.zeros_like(q)) for h in range(n_heads)], axis=0)
    cost_ref[...] = jnp.zeros_like(cost_ref)
    acc_ref[...] = jnp.zeros_like(acc_ref)

    def block(j, diag):
        start = pl.multiple_of(j * tq, tq)
        k = k_ref[pl.ds(start, tq), :]
        v = v_ref[pl.ds(start, tq), :]
        tri = tri_ref[...]
        z = lax.dot_general(q_all, k, _NT, preferred_element_type=F32)
        tail = jnp.log2(1.0 + jnp.exp2(-jnp.abs(z)))
        cost_stay = jnp.maximum(z, 0.0) + tail
        if diag:
            past = jnp.concatenate([col < row] * n_heads, axis=0)
            cost_stay = jnp.where(past, cost_stay, 0.0)
        hi, lo = _split_bf16(cost_stay)
        between = (jnp.dot(hi, tri, preferred_element_type=F32)
                   + jnp.dot(lo, tri, preferred_element_type=F32))
        w = jnp.exp2(jnp.minimum(z, 0.0) - tail - between - cost_ref[...])
        if diag:
            w = jnp.where(past, w, 0.0)
        acc_ref[...] += jnp.dot(w.astype(BF16), v, preferred_element_type=F32)
        cost_ref[...] += jnp.sum(_lane_sums(cost_stay), axis=-1, keepdims=True)

    block(i, True)

    def cond(state):
        j, live = state
        return jnp.logical_and(j >= 0, live > 0)

    def body(state):
        j, _ = state
        block(j, False)
        live = (jnp.min(cost_ref[...]) < SB_ZERO_COST).astype(jnp.int32)
        return j - 1, live

    lax.while_loop(cond, body, (i - 1, jnp.int32(1)))
    out = jnp.zeros((tq, LANES), F32)
    for h in range(n_heads):
        out = out + jnp.where((lane // HEAD_DIM) == h, acc_ref[h * tq:(h + 1) * tq, :], 0.0)
    o_ref[...] = out.astype(o_ref.dtype)


def sb_attention(proj, batch, seq):
    t = proj.shape[0]
    tq = min(TQ_SB, seq)
    nq = seq // tq
    n_pair = N_HEADS_SB * HEAD_DIM // LANES
    return pl.pallas_call(
        functools.partial(_sb_kernel, tq=tq),
        grid=(batch, n_pair, nq),
        in_specs=[pl.BlockSpec((tq, LANES), lambda b, p, i: (b * nq + i, p)),
                  pl.BlockSpec((seq, LANES), lambda b, p, i: (b, n_pair + p)),
                  pl.BlockSpec((seq, LANES), lambda b, p, i: (b, 2 * n_pair + p))],
        out_specs=pl.BlockSpec((tq, LANES), lambda b, p, i: (b * nq + i, p)),
        out_shape=jax.ShapeDtypeStruct((t, n_pair * LANES), BF16),
        scratch_shapes=[pltpu.VMEM((tq, tq), BF16), pltpu.VMEM((LANES // HEAD_DIM * tq, 1), F32),
                        pltpu.VMEM((LANES // HEAD_DIM * tq, LANES), F32)],
        compiler_params=_params("parallel", "parallel", "arbitrary"),
        name="sb_attention",
    )(proj, proj, proj)


def _diff_kernel(bound_ref, q_ref, k_ref, v_ref, lam_ref, subln_ref, o_ref, m_ref, l_ref, acc_ref, *,
                 tq, lam_init):
    i = pl.program_id(2)
    bound = bound_ref[0, 0]
    lane = lax.broadcasted_iota(jnp.int32, (1, LANES), 1)
    q = q_ref[...]
    q_maps = [jnp.where((lane // HEAD_DIM) == m, q, jnp.zeros_like(q)) for m in range(2)]
    l_ref[...] = jnp.zeros_like(l_ref)
    acc_ref[...] = jnp.zeros_like(acc_ref)

    def scores(m, k, diag, r0=0, c0=0):
        nr, nk = tq - r0, k.shape[0]
        s = lax.dot_general(q_maps[m][r0:], k, _NT, preferred_element_type=F32)
        if diag:
            row = r0 + lax.broadcasted_iota(jnp.int32, (nr, nk), 0)
            col = c0 + lax.broadcasted_iota(jnp.int32, (nr, nk), 1)
            s = jnp.where((col // CHUNK) <= (row // CHUNK), s, -jnp.inf)
        return s

    def fixed_part(j, diag, r0, c0, nk):
        start = pl.multiple_of(j * tq + c0, nk)
        k = k_ref[pl.ds(start, nk), :]
        v = v_ref[pl.ds(start, nk), :]
        for m in range(2):
            p = jnp.exp2(scores(m, k, diag, r0, c0) - bound)
            l_ref[m, r0:] += _lane_sums(p)
            acc_ref[m, r0:] += jnp.dot(p.astype(BF16), v, preferred_element_type=F32)

    def fixed_step(j, diag):
        if diag:
            fixed_part(j, True, 0, 0, tq // 2)
            fixed_part(j, True, tq // 2, tq // 2, tq // 2)
        else:
            fixed_part(j, False, 0, 0, tq)

    def online_step(j, diag):
        start = pl.multiple_of(j * tq, tq)
        k = k_ref[pl.ds(start, tq), :]
        v = v_ref[pl.ds(start, tq), :]
        for m in range(2):
            s = scores(m, k, diag)
            m_old = m_ref[m]
            m_new = jnp.maximum(m_old, jnp.max(s, axis=-1, keepdims=True))
            p = jnp.exp2(s - m_new)
            alpha = jnp.exp2(m_old - m_new)
            l_ref[m] = alpha * l_ref[m] + _lane_sums(p)
            acc_ref[m] = alpha * acc_ref[m] + jnp.dot(p.astype(BF16), v, preferred_element_type=F32)
            m_ref[m] = m_new

    def sweep(step):
        def body(j, c):
            step(j, False)
            return c
        lax.fori_loop(0, i, body, 0)
        step(i, True)

    @pl.when(bound <= FIXED_STABILISER_MAX)
    def _():
        sweep(fixed_step)

    @pl.when(jnp.logical_not(bound <= FIXED_STABILISER_MAX))
    def _():
        m_ref[...] = jnp.full_like(m_ref, -jnp.inf)
        sweep(online_step)

    lam_vecs = lam_ref[...]
    lam = (jnp.exp(jnp.sum(lam_vecs[0:1] * lam_vecs[1:2], axis=-1, keepdims=True))
           - jnp.exp(jnp.sum(lam_vecs[2:3] * lam_vecs[3:4], axis=-1, keepdims=True)) + lam_init)
    norm = [jnp.sum(l_ref[m], axis=-1, keepdims=True) for m in range(2)]
    o = acc_ref[0] / norm[0] - lam * (acc_ref[1] / norm[1])
    o_ref[...] = (_rms(o, subln_ref[...]) * (1.0 - lam_init)).astype(o_ref.dtype)


def diff_attention(qd, kd, proj, score_bound, lam_vecs, subln, lam_init, batch, seq):
    t = qd.shape[0]
    tq = min(TQ_DIFF, seq)
    nq = seq // tq
    v_col = (3 * N_HEADS_SB * HEAD_DIM + 2 * N_HEADS_DIFF * 2 * HEAD_DIM) // LANES
    return pl.pallas_call(
        functools.partial(_diff_kernel, tq=tq, lam_init=lam_init),
        grid=(batch, N_HEADS_DIFF, nq),
        in_specs=[pl.BlockSpec(memory_space=pltpu.SMEM),
                  pl.BlockSpec((tq, LANES), lambda b, h, i: (b * nq + i, h)),
                  pl.BlockSpec((seq, LANES), lambda b, h, i: (b, h)),
                  pl.BlockSpec((seq, LANES), lambda b, h, i: (b, v_col + h)),
                  pl.BlockSpec((4, HEAD_DIM), lambda b, h, i: (0, 0)),
                  pl.BlockSpec((1, LANES), lambda b, h, i: (0, 0))],
        out_specs=pl.BlockSpec((tq, LANES), lambda b, h, i: (b * nq + i, h)),
        out_shape=jax.ShapeDtypeStruct((t, N_HEADS_DIFF * LANES), BF16),
        scratch_shapes=[pltpu.VMEM((2, tq, 1), F32), pltpu.VMEM((2, tq, LANES), F32),
                        pltpu.VMEM((2, tq, LANES), F32)],
        compiler_params=_params("parallel", "parallel", "arbitrary"),
        name="diff_attention",
    )(score_bound.reshape(1, 1), qd, kd, proj, lam_vecs, subln.reshape(1, LANES))


def _band_kernel(bound_ref, q_ref, k_ref, v_ref, bias_ref, o_ref, *, seq):
    bound = bound_ref[0, 0]
    lane = lax.broadcasted_iota(jnp.int32, (1, LANES), 1)
    left = BAND_WINDOW - TQ_BAND

    def tile(t0, k_start, n_keys, fixed):
        q = q_ref[pl.ds(t0, TQ_BAND), :]
        k = k_ref[pl.ds(k_start, n_keys), :]
        v = v_ref[pl.ds(k_start, n_keys), :]
        out = jnp.zeros((TQ_BAND, LANES), F32)
        for half in range(LANES // HEAD_DIM):
            in_head = (lane // HEAD_DIM) == half
            qh = jnp.where(in_head, q, jnp.zeros_like(q))
            s = lax.dot_general(qh, k, _NT, preferred_element_type=F32)
            s = s + bias_ref[half, :, BAND_WINDOW - n_keys:]
            p = jnp.exp2(s - (bound if fixed else jnp.max(s, axis=-1, keepdims=True)))
            o = jnp.dot(p.astype(BF16), v, preferred_element_type=F32)
            norm = jnp.sum(_lane_sums(p), axis=-1, keepdims=True)
            out = out + jnp.where(in_head, o / norm, 0.0)
        o_ref[pl.ds(t0, TQ_BAND), :] = out.astype(o_ref.dtype)

    def sweep(fixed):
        n_tiles = seq // TQ_BAND
        n_edge = min(left // TQ_BAND, n_tiles)
        for i in range(n_edge):
            tile(i * TQ_BAND, 0, (i + 1) * TQ_BAND, fixed)

        def body(i, c):
            t0 = pl.multiple_of(i * TQ_BAND, TQ_BAND)
            tile(t0, pl.multiple_of(t0 - left, TQ_BAND), BAND_WINDOW, fixed)
            return c

        lax.fori_loop(n_edge, n_tiles, body, 0, unroll=BAND_UNROLL)

    @pl.when(bound <= FIXED_STABILISER_MAX)
    def _():
        sweep(True)

    @pl.when(jnp.logical_not(bound <= FIXED_STABILISER_MAX))
    def _():
        sweep(False)


def band_bias_table(rel_table):
    left = BAND_WINDOW - TQ_BAND
    period = TQ_BAND + BAND_WINDOW
    k = np.arange(period)
    k = np.where(k < BAND_WINDOW, k, k - period)
    by_offset = rel_table.astype(F32)[:, np.clip(left - k, -REL_CLIP, REL_CLIP) + REL_CLIP]
    skew = jnp.tile(by_offset, (1, TQ_BAND))[:, :TQ_BAND * (period - 1)]
    bias = skew.reshape(-1, TQ_BAND, period - 1)[:, :, :BAND_WINDOW]
    q_chunk = np.arange(TQ_BAND)[:, None] // CHUNK
    k_chunk = np.arange(BAND_WINDOW)[None, :] // CHUNK
    in_band = (k_chunk >= q_chunk) & (k_chunk <= q_chunk + LEFT_CHUNKS)
    return jnp.where(jnp.asarray(in_band)[None], bias * LOG2E, -jnp.inf)


def score_bound(gq, gk, bias_max=0.0):
    qk = HEAD_DIM * ATTN_SCALE * jnp.max(jnp.abs(gq)) * jnp.max(jnp.abs(gk))
    return ((qk + bias_max) * (LOG2E * BOUND_SLACK)).astype(F32)


def band_attention(qn, kn, proj, bias, score_bound, batch, seq):
    t, width = qn.shape
    n_pair = width // LANES
    seq_block = lambda col0: pl.BlockSpec((seq, LANES), lambda b, p: (b, col0 + p))
    return pl.pallas_call(
        functools.partial(_band_kernel, seq=seq),
        grid=(batch, n_pair),
        in_specs=[pl.BlockSpec(memory_space=pltpu.SMEM),
                  seq_block(0), seq_block(0), seq_block(2 * n_pair),
                  pl.BlockSpec((LANES // HEAD_DIM, TQ_BAND, BAND_WINDOW), lambda b, p: (p, 0, 0))],
        out_specs=seq_block(0),
        out_shape=jax.ShapeDtypeStruct((t, width), BF16),
        compiler_params=_params("parallel", "parallel"),
        name="band_attention",
    )(score_bound.reshape(1, 1), qn, kn, proj, bias)


def _out_proj_kernel(*refs):
    x_ref, *part_refs, w_ref, o_ref = refs
    acc = x_ref[...]
    row = 0
    for part in part_refs:
        d = part.shape[1]
        acc = acc + jnp.dot(part[...], w_ref[row:row + d, :], preferred_element_type=F32)
        row += d
    o_ref[...] = acc


def out_proj_residual(x, parts, w):
    t, d = x.shape
    tm = min(TM_PROJ, t)
    return pl.pallas_call(
        _out_proj_kernel,
        grid=(t // tm,),
        in_specs=([pl.BlockSpec((tm, d), lambda i: (i, 0))]
                  + [pl.BlockSpec((tm, p.shape[1]), lambda i: (i, 0)) for p in parts]
                  + [pl.BlockSpec(w.shape, lambda i: (0, 0))]),
        out_specs=pl.BlockSpec((tm, d), lambda i: (i, 0)),
        out_shape=jax.ShapeDtypeStruct((t, d), F32),
        compiler_params=_params("parallel"),
        name="out_proj_residual",
    )(x, *parts, w)


def _swiglu_kernel(x_ref, g_ref, wg_ref, wu_ref, wd_ref, o_ref, h_ref, acc_ref):
    j = pl.program_id(1)

    @pl.when(j == 0)
    def _():
        x = x_ref[...]
        h_ref[...] = _rms(x, g_ref[...]).astype(BF16)
        acc_ref[...] = x

    h = h_ref[...]
    gate = jnp.dot(h, wg_ref[...], preferred_element_type=F32)
    up = jnp.dot(h, wu_ref[...], preferred_element_type=F32)
    act = (gate * jax.nn.sigmoid(gate) * up).astype(BF16)
    acc_ref[...] += jnp.dot(act, wd_ref[...], preferred_element_type=F32)

    @pl.when(j == pl.num_programs(1) - 1)
    def _():
        o_ref[...] = acc_ref[...]


def swiglu_residual(x, g, wg, wu, wd):
    t, d = x.shape
    f = wg.shape[1]
    tm = min(TM_PROJ, t)
    tf = TF_DENSE if f % TF_DENSE == 0 else f
    return pl.pallas_call(
        _swiglu_kernel,
        grid=(t // tm, f // tf),
        in_specs=[pl.BlockSpec((tm, d), lambda i, j: (i, 0)),
                  pl.BlockSpec((1, d), lambda i, j: (0, 0)),
                  pl.BlockSpec((d, tf), lambda i, j: (0, j)),
                  pl.BlockSpec((d, tf), lambda i, j: (0, j)),
                  pl.BlockSpec((tf, d), lambda i, j: (j, 0))],
        out_specs=pl.BlockSpec((tm, d), lambda i, j: (i, 0)),
        out_shape=jax.ShapeDtypeStruct((t, d), F32),
        scratch_shapes=[pltpu.VMEM((tm, d), BF16), pltpu.VMEM((tm, d), F32)],
        compiler_params=_params("parallel", "arbitrary"),
        name="swiglu_residual",
    )(x, g.reshape(1, d), wg, wu, wd)


def _router_kernel(x_ref, g_ref, wr_ref, h_ref, pick_ref, gate_ref, count_ref, carry_ref, *, tm):
    i = pl.program_id(0)

    @pl.when(i == 0)
    def _():
        carry_ref[...] = jnp.zeros_like(carry_ref)

    h = _rms(x_ref[...], g_ref[...])
    h_ref[...] = h
    logits = lax.dot_general(wr_ref[...], h, _NT, preferred_element_type=F32,
                             precision=lax.Precision.HIGHEST)
    e_idx = lax.broadcasted_iota(jnp.int32, (N_EXPERTS, tm), 0)

    def top1(vals):
        best = jnp.max(vals, axis=0, keepdims=True)
        arg = jnp.min(jnp.where(vals == best, e_idx, N_EXPERTS), axis=0, keepdims=True)
        return best, arg

    v1, i1 = top1(logits)
    v2, i2 = top1(jnp.where(e_idx == i1, -jnp.inf, logits))
    ratio = jnp.exp(v2 - v1)
    g1 = 1.0 / (1.0 + ratio)
    g2 = ratio / (1.0 + ratio)

    picked = jnp.where((e_idx == i1) | (e_idx == i2), 1.0, 0.0)
    a = lax.broadcasted_iota(jnp.int32, (tm, tm), 0)
    b = lax.broadcasted_iota(jnp.int32, (tm, tm), 1)
    before = jnp.where(a < b, 1.0, 0.0).astype(BF16)
    rank = carry_ref[...] + jnp.dot(picked.astype(BF16), before, preferred_element_type=F32)
    rank = rank.astype(jnp.int32)
    r1 = jnp.sum(jnp.where(e_idx == i1, rank, 0), axis=0, keepdims=True)
    r2 = jnp.sum(jnp.where(e_idx == i2, rank, 0), axis=0, keepdims=True)
    carry_ref[...] += jnp.sum(picked, axis=1, keepdims=True)

    pick_ref[...] = jnp.zeros_like(pick_ref)
    gate_ref[...] = jnp.zeros_like(gate_ref)
    for r, val in enumerate((i1, i2, r1, r2)):
        pick_ref[r:r + 1, :] = val
    for r, val in enumerate((g1, g2)):
        gate_ref[r:r + 1, :] = val
    count_ref[...] = jnp.broadcast_to(carry_ref[...], count_ref.shape).astype(jnp.int32)


def router(x, g, w_router):
    t, d = x.shape
    tm = min(TM_ROUTE, t)
    return pl.pallas_call(
        functools.partial(_router_kernel, tm=tm),
        grid=(t // tm,),
        in_specs=[pl.BlockSpec((tm, d), lambda i: (i, 0)),
                  pl.BlockSpec((1, d), lambda i: (0, 0)),
                  pl.BlockSpec((N_EXPERTS, d), lambda i: (0, 0))],
        out_specs=[pl.BlockSpec((tm, d), lambda i: (i, 0)),
                   pl.BlockSpec((8, tm), lambda i: (0, i)),
                   pl.BlockSpec((8, tm), lambda i: (0, i)),
                   pl.BlockSpec((N_EXPERTS, LANES), lambda i: (0, 0))],
        out_shape=[jax.ShapeDtypeStruct((t, d), F32),
                   jax.ShapeDtypeStruct((8, t), jnp.int32),
                   jax.ShapeDtypeStruct((8, t), F32),
                   jax.ShapeDtypeStruct((N_EXPERTS, LANES), jnp.int32)],
        scratch_shapes=[pltpu.VMEM((N_EXPERTS, 1), F32)],
        compiler_params=_params("arbitrary"),
        name="router",
    )(x, g.reshape(1, d), w_router.T)


def _row_copy(src_ref, src_row, dst_ref, dst_row, sem):
    return pltpu.make_async_copy(src_ref.at[pl.ds(src_row, 1)], dst_ref.at[pl.ds(dst_row, 1)], sem)


def _dispatch_kernel(dest_ref, h_ref, init_ref, xs_ref, sem, *, tm):
    del init_ref

    def issue(t, c):
        for k in range(TOP_K):
            _row_copy(h_ref, t, xs_ref, dest_ref[0, k, t], sem).start()
        return c

    lax.fori_loop(0, tm, issue, 0, unroll=ROW_DMA_UNROLL)
    for k in range(TOP_K):
        pltpu.make_async_copy(h_ref, xs_ref.at[pl.ds(0, tm)], sem).wait()


def dispatch_rows(h, dest, n_rows):
    t, d = h.shape
    tm = min(TM_ROWS, t)
    return pl.pallas_call(
        functools.partial(_dispatch_kernel, tm=tm),
        grid=(t // tm,),
        in_specs=[pl.BlockSpec((1, TOP_K, tm), lambda i: (i, 0, 0), memory_space=pltpu.SMEM),
                  pl.BlockSpec((tm, d), lambda i: (i, 0)),
                  pl.BlockSpec(memory_space=pl.ANY)],
        out_specs=pl.BlockSpec(memory_space=pl.ANY),
        out_shape=jax.ShapeDtypeStruct((n_rows, d), h.dtype),
        scratch_shapes=[pltpu.SemaphoreType.DMA(())],
        input_output_aliases={2: 0},
        compiler_params=_params("arbitrary"),
        name="dispatch_rows",
    )(dest, h, jnp.zeros((n_rows, d), h.dtype))


def _combine_kernel(dest_ref, x_ref, gate_ref, ys_ref, o_ref, buf_ref, sem, *, tm):
    def issue(t, c):
        for k in range(TOP_K):
            _row_copy(ys_ref, dest_ref[0, k, t], buf_ref.at[k], t, sem).start()
        return c

    lax.fori_loop(0, tm, issue, 0, unroll=ROW_DMA_UNROLL)
    for k in range(TOP_K):
        pltpu.make_async_copy(ys_ref.at[pl.ds(0, tm)], buf_ref.at[k], sem).wait()
    gate = gate_ref[...]
    o_ref[...] = x_ref[...] + gate[:, 0:1] * buf_ref[0] + gate[:, 1:2] * buf_ref[1]


def combine_rows(x, gates, ys, dest):
    t, d = x.shape
    tm = min(TM_ROWS, t)
    return pl.pallas_call(
        functools.partial(_combine_kernel, tm=tm),
        grid=(t // tm,),
        in_specs=[pl.BlockSpec((1, TOP_K, tm), lambda i: (i, 0, 0), memory_space=pltpu.SMEM),
                  pl.BlockSpec((tm, d), lambda i: (i, 0)),
                  pl.BlockSpec((tm, TOP_K), lambda i: (i, 0)),
                  pl.BlockSpec(memory_space=pl.ANY)],
        out_specs=pl.BlockSpec((tm, d), lambda i: (i, 0)),
        out_shape=jax.ShapeDtypeStruct((t, d), F32),
        scratch_shapes=[pltpu.VMEM((TOP_K, tm, d), F32), pltpu.SemaphoreType.DMA(())],
        compiler_params=_params("arbitrary"),
        name="combine_rows",
    )(dest, x, gates, ys)


def _expert_ffn_kernel(expert_ref, live_ref, xs_ref, wg_ref, wu_ref, wd_ref, o_ref, h_ref, acc_ref):
    del expert_ref
    i, j = pl.program_id(0), pl.program_id(1)
    live = live_ref[i] > 0

    @pl.when(j == 0)
    def _():
        h_ref[...] = xs_ref[...].astype(BF16)
        acc_ref[...] = jnp.zeros_like(acc_ref)

    @pl.when(live)
    def _():
        h = h_ref[...]
        gate = jnp.dot(h, wg_ref[0], preferred_element_type=F32)
        up = jnp.dot(h, wu_ref[0], preferred_element_type=F32)
        act = (gate * jax.nn.sigmoid(gate) * up).astype(BF16)
        acc_ref[...] += jnp.dot(act, wd_ref[0], preferred_element_type=F32)

    @pl.when(j == pl.num_programs(1) - 1)
    def _():
        o_ref[...] = acc_ref[...]


def expert_ffn(xs, tile_expert, tile_live, wg, wu, wd):
    r, d = xs.shape
    f = wg.shape[2]
    tm = TM_EXPERT
    tf = TF_EXPERT if f % TF_EXPERT == 0 else f
    grid_spec = pltpu.PrefetchScalarGridSpec(
        num_scalar_prefetch=2,
        grid=(r // tm, f // tf),
        in_specs=[pl.BlockSpec((tm, d), lambda i, j, e, v: (i, 0)),
                  pl.BlockSpec((1, d, tf), lambda i, j, e, v: (e[i], 0, j)),
                  pl.BlockSpec((1, d, tf), lambda i, j, e, v: (e[i], 0, j)),
                  pl.BlockSpec((1, tf, d), lambda i, j, e, v: (e[i], j, 0))],
        out_specs=pl.BlockSpec((tm, d), lambda i, j, e, v: (i, 0)),
        scratch_shapes=[pltpu.VMEM((tm, d), BF16), pltpu.VMEM((tm, d), F32)])
    return pl.pallas_call(
        _expert_ffn_kernel,
        grid_spec=grid_spec,
        out_shape=jax.ShapeDtypeStruct((r, d), F32),
        compiler_params=_params("parallel", "arbitrary"),
        name="expert_ffn",
    )(tile_expert, tile_live, xs, wg, wu, wd)


def moe_residual(x, g, w_router, wg, wu, wd):
    t, d = x.shape
    tm = TM_EXPERT
    n_rows = TOP_K * t + N_EXPERTS * tm
    n_tiles = n_rows // tm
    h, picks, gates, counts = router(x, g, w_router)

    counts = counts[:, 0]
    padded = (counts + tm - 1) // tm * tm
    ends = jnp.cumsum(padded)
    starts = ends - padded
    experts, ranks = picks[0:TOP_K], picks[TOP_K:2 * TOP_K]
    dest = jnp.sum(jnp.where(experts[None] == jnp.arange(N_EXPERTS)[:, None, None],
                             starts[:, None, None], 0), axis=0) + ranks
    tile_start = jnp.arange(n_tiles, dtype=jnp.int32) * tm
    tile_expert = jnp.minimum(jnp.sum(tile_start[:, None] >= ends[None, :], axis=1),
                              N_EXPERTS - 1).astype(jnp.int32)
    tile_live = (tile_start < ends[-1]).astype(jnp.int32)

    tr = min(TM_ROWS, t)
    dest_tiles = dest.reshape(TOP_K, t // tr, tr).transpose(1, 0, 2)
    xs = dispatch_rows(h, dest_tiles, n_rows)
    ys = expert_ffn(xs, tile_expert, tile_live, wg, wu, wd)
    return combine_rows(x, gates[0:TOP_K].T, ys, dest_tiles)


def kernel(x, positions, ev_attn_norm, ev_w_in, ev_q_norm, ev_k_norm, ev_lambda_q1, ev_lambda_k1, ev_lambda_q2, ev_lambda_k2, ev_subln, ev_w_out, ev_ffn_norm, ev_w_gate, ev_w_up, ev_w_down, od_attn_norm, od_w_qkv, od_q_norm, od_k_norm, od_rel_bias, od_w_out, od_ffn_norm, od_router, od_we_gate, od_we_up, od_we_down):
    batch, seq, d_model = x.shape
    depth = ev_attn_norm.shape[0] + od_attn_norm.shape[0]
    bf = lambda w: w.astype(BF16)
    x = x.reshape(batch * seq, d_model)
    d_sb = N_HEADS_SB * HEAD_DIM
    d_diff = N_HEADS_DIFF * 2 * HEAD_DIM
    for layer in range(depth):
        i = layer // 2
        if layer % 2 == 0:
            col_scale = jnp.where(jnp.arange(ev_w_in.shape[2]) < d_sb, ATTN_SCALE * LOG2E, 1.0)
            proj = norm_proj(x, ev_attn_norm[i], bf(ev_w_in[i] * col_scale))
            o_sb = sb_attention(proj, batch, seq)
            qd, kd = qk_prep(proj, 3 * d_sb // d_diff, 3 * d_sb // d_diff + 1, d_diff,
                             ev_q_norm[i], ev_k_norm[i], positions)
            lam_init = 0.8 - 0.6 * math.exp(-0.3 * layer)
            lam_vecs = jnp.stack([ev_lambda_q1[i], ev_lambda_k1[i], ev_lambda_q2[i], ev_lambda_k2[i]])
            o_df = diff_attention(qd, kd, proj, score_bound(ev_q_norm[i], ev_k_norm[i]),
                                  lam_vecs.astype(F32), ev_subln[i], lam_init, batch, seq)
            x = out_proj_residual(x, [o_sb, o_df], bf(ev_w_out[i]))
            x = swiglu_residual(x, ev_ffn_norm[i], bf(ev_w_gate[i]), bf(ev_w_up[i]), bf(ev_w_down[i]))
        else:
            proj = norm_proj(x, od_attn_norm[i], bf(od_w_qkv[i]))
            qn, kn = qk_prep(proj, 0, 1, d_model, od_q_norm[i], od_k_norm[i])
            bound = score_bound(od_q_norm[i], od_k_norm[i], jnp.max(jnp.abs(od_rel_bias[i])))
            o_band = band_attention(qn, kn, proj, band_bias_table(od_rel_bias[i]), bound, batch, seq)
            x = out_proj_residual(x, [o_band], bf(od_w_out[i]))
            x = moe_residual(x, od_ffn_norm[i], od_router[i], bf(od_we_gate[i]), bf(od_we_up[i]),
                             bf(od_we_down[i]))
    return x.reshape(batch, seq, d_model)
```

```python
import functools
import math

import numpy as np
import jax
import jax.numpy as jnp
from jax import lax
from jax.experimental import pallas as pl
from jax.experimental.pallas import tpu as pltpu

F32 = jnp.float32
BF16 = jnp.bfloat16

HEAD_DIM = 64
CHUNK = 64
N_HEADS_SB = 8
N_HEADS_DIFF = 4
ROPE_THETA = 500000.0
ROT_DIM = HEAD_DIM // 4
LEFT_CHUNKS = 8
REL_CLIP = 128
N_EXPERTS = 8
TOP_K = 2
RMS_EPS = 1e-6
ATTN_SCALE = HEAD_DIM ** -0.5

LANES = 128
VMEM_LIMIT_BYTES = 48 * 1024 * 1024

SB_ZERO_COST = 160.0

LOG2E = math.log2(math.e)
FIXED_STABILISER_MAX = 60.0
BOUND_SLACK = 1.02

TM_PROJ = 512
TQ_SB = 256
TQ_DIFF = 1024
TQ_BAND = 256
BAND_WINDOW = TQ_BAND + LEFT_CHUNKS * CHUNK
BAND_UNROLL = 5
TF_DENSE = 1408
TF_EXPERT = 1792
TM_EXPERT = 512
TM_ROUTE = 512
TM_ROWS = 256
ROW_DMA_UNROLL = 8

_NT = (((1,), (1,)), ((), ()))


def _params(*sem):
    return pltpu.CompilerParams(dimension_semantics=sem, vmem_limit_bytes=VMEM_LIMIT_BYTES)


def _rms(x, g):
    return x * lax.rsqrt(jnp.mean(x * x, axis=-1, keepdims=True) + RMS_EPS) * g


def _split_bf16(x):
    hi = x.astype(BF16)
    lo = (x - hi.astype(F32)).astype(BF16)
    return hi, lo


def _lane_sums(p):
    part = p[:, 0:LANES]
    for c in range(LANES, p.shape[1], LANES):
        part = part + p[:, c:c + LANES]
    return part


def _norm_proj_kernel(x_ref, g_ref, w_ref, o_ref, *, n_chunk):
    h = _rms(x_ref[...], g_ref[...]).astype(BF16)
    for c in range(0, o_ref.shape[1], n_chunk):
        o_ref[:, c:c + n_chunk] = jnp.dot(
            h, w_ref[:, c:c + n_chunk], preferred_element_type=F32).astype(o_ref.dtype)


def norm_proj(x, g, w):
    t, d = x.shape
    n = w.shape[1]
    tm = min(TM_PROJ, t)
    return pl.pallas_call(
        functools.partial(_norm_proj_kernel, n_chunk=512),
        grid=(t // tm,),
        in_specs=[pl.BlockSpec((tm, d), lambda i: (i, 0)),
                  pl.BlockSpec((1, d), lambda i: (0, 0)),
                  pl.BlockSpec((d, n), lambda i: (0, 0))],
        out_specs=pl.BlockSpec((tm, n), lambda i: (i, 0)),
        out_shape=jax.ShapeDtypeStruct((t, n), BF16),
        compiler_params=_params("parallel"),
        name="norm_proj",
    )(x, g.reshape(1, d), w)


def _head_mean_sq(x, g_ref):
    hi, lo = _split_bf16(x * x)
    g = g_ref[...]
    return jnp.dot(hi, g, preferred_element_type=F32) + jnp.dot(lo, g, preferred_element_type=F32)


def _qk_prep_kernel(*refs, rope):
    if rope:
        q_ref, k_ref, gq_ref, gk_ref, grp_ref, pos_ref, invf_ref, qo_ref, ko_ref = refs
        ang = pos_ref[...] * invf_ref[...]
        d = lax.broadcasted_iota(jnp.int32, (1, LANES), 1) % HEAD_DIM
        cos = jnp.cos(ang)
        sin = jnp.sin(ang)
        coef_fwd = jnp.where(d < ROT_DIM // 2, -sin, 0.0)
        coef_bwd = jnp.where((d >= ROT_DIM // 2) & (d < ROT_DIM), sin, 0.0)
    else:
        q_ref, k_ref, gq_ref, gk_ref, grp_ref, qo_ref, ko_ref = refs
    width = q_ref.shape[1]
    for src, gain, dst, scale in ((q_ref, gq_ref, qo_ref, ATTN_SCALE * LOG2E), (k_ref, gk_ref, ko_ref, 1.0)):
        for c in range(0, width, LANES):
            x = src[:, c:c + LANES].astype(F32)
            y = x * lax.rsqrt(_head_mean_sq(x, grp_ref) + RMS_EPS) * gain[...]
            if rope:
                half = ROT_DIM // 2
                y = (y * cos + pltpu.roll(y, LANES - half, 1) * coef_fwd
                     + pltpu.roll(y, half, 1) * coef_bwd)
            dst[:, c:c + LANES] = (y * scale).astype(dst.dtype)


def qk_prep(proj, q_col, k_col, width, gq, gk, pos=None):
    t = proj.shape[0]
    tm = min(TM_PROJ, t)
    rope = pos is not None
    grp = np.kron(np.eye(LANES // HEAD_DIM), np.full((HEAD_DIM, HEAD_DIM), 1.0 / HEAD_DIM))
    vec = lambda g: jnp.tile(g.astype(F32), LANES // HEAD_DIM).reshape(1, LANES)
    args = [proj, proj, vec(gq), vec(gk), jnp.asarray(grp, BF16)]
    const = lambda shape: pl.BlockSpec(shape, lambda i: (0, 0))
    in_specs = [pl.BlockSpec((tm, width), lambda i: (i, q_col)),
                pl.BlockSpec((tm, width), lambda i: (i, k_col)),
                const((1, LANES)), const((1, LANES)), const((LANES, LANES))]
    if rope:
        inv_freq = ROPE_THETA ** (-jnp.arange(0, ROT_DIM, 2, dtype=F32) / ROT_DIM)
        per_head = jnp.concatenate([inv_freq, inv_freq, jnp.zeros((HEAD_DIM - ROT_DIM,), F32)])
        args += [pos.astype(F32).reshape(t, 1), jnp.tile(per_head, LANES // HEAD_DIM).reshape(1, LANES)]
        in_specs += [pl.BlockSpec((tm, 1), lambda i: (i, 0)), const((1, LANES))]
    out = jax.ShapeDtypeStruct((t, width), BF16)
    return pl.pallas_call(
        functools.partial(_qk_prep_kernel, rope=rope),
        grid=(t // tm,),
        in_specs=in_specs,
        out_specs=[pl.BlockSpec((tm, width), lambda i: (i, 0))] * 2,
        out_shape=[out, out],
        compiler_params=_params("parallel"),
        name="qk_prep_rope" if rope else "qk_prep",
    )(*args)


def _sb_kernel(q_ref, k_ref, v_ref, o_ref, tri_ref, cost_ref, acc_ref, *, tq):
    i = pl.program_id(2)
    n_heads = LANES // HEAD_DIM
    lane = lax.broadcasted_iota(jnp.int32, (1, LANES), 1)
    row = lax.broadcasted_iota(jnp.int32, (tq, tq), 0)
    col = lax.broadcasted_iota(jnp.int32, (tq, tq), 1)
    tri_ref[...] = jnp.where(row > col, 1.0, 0.0).astype(BF16)
    q = q_ref[...]
    q_all = jnp.concatenate(
        [jnp.where((lane // HEAD_DIM) == h, q, jnp.zeros_like(q)) for h in range(n_heads)], axis=0)
    cost_ref[...] = jnp.zeros_like(cost_ref)
    acc_ref[...] = jnp.zeros_like(acc_ref)

    def block(j, diag):
        start = pl.multiple_of(j * tq, tq)
        k = k_ref[pl.ds(start, tq), :]
        v = v_ref[pl.ds(start, tq), :]
        tri = tri_ref[...]
        z = lax.dot_general(q_all, k, _NT, preferred_element_type=F32)
        tail = jnp.log2(1.0 + jnp.exp2(-jnp.abs(z)))
        cost_stay = jnp.maximum(z, 0.0) + tail
        if diag:
            past = jnp.concatenate([col < row] * n_heads, axis=0)
            cost_stay = jnp.where(past, cost_stay, 0.0)
        between = jnp.dot(cost_stay.astype(BF16), tri, preferred_element_type=F32)
        w = jnp.exp2(jnp.minimum(z, 0.0) - tail - between - cost_ref[...])
        if diag:
            w = jnp.where(past, w, 0.0)
        acc_ref[...] += jnp.dot(w.astype(BF16), v, preferred_element_type=F32)
        cost_ref[...] += jnp.sum(_lane_sums(cost_stay), axis=-1, keepdims=True)

    block(i, True)

    def cond(state):
        j, live = state
        return jnp.logical_and(j >= 0, live > 0)

    def body(state):
        j, _ = state
        block(j, False)
        live = (jnp.min(cost_ref[...]) < SB_ZERO_COST).astype(jnp.int32)
        return j - 1, live

    lax.while_loop(cond, body, (i - 1, jnp.int32(1)))
    out = jnp.zeros((tq, LANES), F32)
    for h in range(n_heads):
        out = out + jnp.where((lane // HEAD_DIM) == h, acc_ref[h * tq:(h + 1) * tq, :], 0.0)
    o_ref[...] = out.astype(o_ref.dtype)


def sb_attention(proj, batch, seq):
    t = proj.shape[0]
    tq = min(TQ_SB, seq)
    nq = seq // tq
    n_pair = N_HEADS_SB * HEAD_DIM // LANES
    return pl.pallas_call(
        functools.partial(_sb_kernel, tq=tq),
        grid=(batch, n_pair, nq),
        in_specs=[pl.BlockSpec((tq, LANES), lambda b, p, i: (b * nq + i, p)),
                  pl.BlockSpec((seq, LANES), lambda b, p, i: (b, n_pair + p)),
                  pl.BlockSpec((seq, LANES), lambda b, p, i: (b, 2 * n_pair + p))],
        out_specs=pl.BlockSpec((tq, LANES), lambda b, p, i: (b * nq + i, p)),
        out_shape=jax.ShapeDtypeStruct((t, n_pair * LANES), BF16),
        scratch_shapes=[pltpu.VMEM((tq, tq), BF16), pltpu.VMEM((LANES // HEAD_DIM * tq, 1), F32),
                        pltpu.VMEM((LANES // HEAD_DIM * tq, LANES), F32)],
        compiler_params=_params("parallel", "parallel", "arbitrary"),
        name="sb_attention",
    )(proj, proj, proj)


def _diff_kernel(bound_ref, q_ref, k_ref, v_ref, lam_ref, subln_ref, o_ref, m_ref, l_ref, acc_ref, *,
                 tq, lam_init):
    i = pl.program_id(2)
    bound = bound_ref[0, 0]
    lane = lax.broadcasted_iota(jnp.int32, (1, LANES), 1)
    q = q_ref[...]
    q_maps = [jnp.where((lane // HEAD_DIM) == m, q, jnp.zeros_like(q)) for m in range(2)]
    l_ref[...] = jnp.zeros_like(l_ref)
    acc_ref[...] = jnp.zeros_like(acc_ref)

    def scores(m, k, diag, r0=0, c0=0):
        nr, nk = tq - r0, k.shape[0]
        s = lax.dot_general(q_maps[m][r0:], k, _NT, preferred_element_type=F32)
        if diag:
            row = r0 + lax.broadcasted_iota(jnp.int32, (nr, nk), 0)
            col = c0 + lax.broadcasted_iota(jnp.int32, (nr, nk), 1)
            s = jnp.where((col // CHUNK) <= (row // CHUNK), s, -jnp.inf)
        return s

    def fixed_part(j, diag, r0, c0, nk):
        start = pl.multiple_of(j * tq + c0, nk)
        k = k_ref[pl.ds(start, nk), :]
        v = v_ref[pl.ds(start, nk), :]
        for m in range(2):
            p = jnp.exp2(scores(m, k, diag, r0, c0) - bound)
            l_ref[m, r0:] += _lane_sums(p)
            acc_ref[m, r0:] += jnp.dot(p.astype(BF16), v, preferred_element_type=F32)

    def fixed_step(j, diag):
        if diag:
            fixed_part(j, True, 0, 0, tq // 2)
            fixed_part(j, True, tq // 2, tq // 2, tq // 2)
        else:
            fixed_part(j, False, 0, 0, tq)

    def online_step(j, diag):
        start = pl.multiple_of(j * tq, tq)
        k = k_ref[pl.ds(start, tq), :]
        v = v_ref[pl.ds(start, tq), :]
        for m in range(2):
            s = scores(m, k, diag)
            m_old = m_ref[m]
            m_new = jnp.maximum(m_old, jnp.max(s, axis=-1, keepdims=True))
            p = jnp.exp2(s - m_new)
            alpha = jnp.exp2(m_old - m_new)
            l_ref[m] = alpha * l_ref[m] + _lane_sums(p)
            acc_ref[m] = alpha * acc_ref[m] + jnp.dot(p.astype(BF16), v, preferred_element_type=F32)
            m_ref[m] = m_new

    def sweep(step):
        def body(j, c):
            step(j, False)
            return c
        lax.fori_loop(0, i, body, 0)
        step(i, True)

    @pl.when(bound <= FIXED_STABILISER_MAX)
    def _():
        sweep(fixed_step)

    @pl.when(jnp.logical_not(bound <= FIXED_STABILISER_MAX))
    def _():
        m_ref[...] = jnp.full_like(m_ref, -jnp.inf)
        sweep(online_step)

    lam_vecs = lam_ref[...]
    lam = (jnp.exp(jnp.sum(lam_vecs[0:1] * lam_vecs[1:2], axis=-1, keepdims=True))
           - jnp.exp(jnp.sum(lam_vecs[2:3] * lam_vecs[3:4], axis=-1, keepdims=True)) + lam_init)
    norm = [jnp.sum(l_ref[m], axis=-1, keepdims=True) for m in range(2)]
    o = acc_ref[0] / norm[0] - lam * (acc_ref[1] / norm[1])
    o_ref[...] = (_rms(o, subln_ref[...]) * (1.0 - lam_init)).astype(o_ref.dtype)


def diff_attention(qd, kd, proj, score_bound, lam_vecs, subln, lam_init, batch, seq):
    t = qd.shape[0]
    tq = min(TQ_DIFF, seq)
    nq = seq // tq
    v_col = (3 * N_HEADS_SB * HEAD_DIM + 2 * N_HEADS_DIFF * 2 * HEAD_DIM) // LANES
    return pl.pallas_call(
        functools.partial(_diff_kernel, tq=tq, lam_init=lam_init),
        grid=(batch, N_HEADS_DIFF, nq),
        in_specs=[pl.BlockSpec(memory_space=pltpu.SMEM),
                  pl.BlockSpec((tq, LANES), lambda b, h, i: (b * nq + i, h)),
                  pl.BlockSpec((seq, LANES), lambda b, h, i: (b, h)),
                  pl.BlockSpec((seq, LANES), lambda b, h, i: (b, v_col + h)),
                  pl.BlockSpec((4, HEAD_DIM), lambda b, h, i: (0, 0)),
                  pl.BlockSpec((1, LANES), lambda b, h, i: (0, 0))],
        out_specs=pl.BlockSpec((tq, LANES), lambda b, h, i: (b * nq + i, h)),
        out_shape=jax.ShapeDtypeStruct((t, N_HEADS_DIFF * LANES), BF16),
        scratch_shapes=[pltpu.VMEM((2, tq, 1), F32), pltpu.VMEM((2, tq, LANES), F32),
                        pltpu.VMEM((2, tq, LANES), F32)],
        compiler_params=_params("parallel", "parallel", "arbitrary"),
        name="diff_attention",
    )(score_bound.reshape(1, 1), qd, kd, proj, lam_vecs, subln.reshape(1, LANES))


def _band_kernel(bound_ref, q_ref, k_ref, v_ref, bias_ref, o_ref, *, seq):
    bound = bound_ref[0, 0]
    lane = lax.broadcasted_iota(jnp.int32, (1, LANES), 1)
    left = BAND_WINDOW - TQ_BAND

    def tile(t0, k_start, n_keys, fixed):
        q = q_ref[pl.ds(t0, TQ_BAND), :]
        k = k_ref[pl.ds(k_start, n_keys), :]
        v = v_ref[pl.ds(k_start, n_keys), :]
        out = jnp.zeros((TQ_BAND, LANES), F32)
        for half in range(LANES // HEAD_DIM):
            in_head = (lane // HEAD_DIM) == half
            qh = jnp.where(in_head, q, jnp.zeros_like(q))
            s = lax.dot_general(qh, k, _NT, preferred_element_type=F32)
            s = s + bias_ref[half, :, BAND_WINDOW - n_keys:]
            p = jnp.exp2(s - (bound if fixed else jnp.max(s, axis=-1, keepdims=True)))
            o = jnp.dot(p.astype(BF16), v, preferred_element_type=F32)
            norm = jnp.sum(_lane_sums(p), axis=-1, keepdims=True)
            out = out + jnp.where(in_head, o / norm, 0.0)
        o_ref[pl.ds(t0, TQ_BAND), :] = out.astype(o_ref.dtype)

    def sweep(fixed):
        n_tiles = seq // TQ_BAND
        n_edge = min(left // TQ_BAND, n_tiles)
        for i in range(n_edge):
            tile(i * TQ_BAND, 0, (i + 1) * TQ_BAND, fixed)

        def body(i, c):
            t0 = pl.multiple_of(i * TQ_BAND, TQ_BAND)
            tile(t0, pl.multiple_of(t0 - left, TQ_BAND), BAND_WINDOW, fixed)
            return c

        lax.fori_loop(n_edge, n_tiles, body, 0, unroll=BAND_UNROLL)

    @pl.when(bound <= FIXED_STABILISER_MAX)
    def _():
        sweep(True)

    @pl.when(jnp.logical_not(bound <= FIXED_STABILISER_MAX))
    def _():
        sweep(False)


def band_bias_table(rel_table):
    left = BAND_WINDOW - TQ_BAND
    period = TQ_BAND + BAND_WINDOW
    k = np.arange(period)
    k = np.where(k < BAND_WINDOW, k, k - period)
    by_offset = rel_table.astype(F32)[:, np.clip(left - k, -REL_CLIP, REL_CLIP) + REL_CLIP]
    skew = jnp.tile(by_offset, (1, TQ_BAND))[:, :TQ_BAND * (period - 1)]
    bias = skew.reshape(-1, TQ_BAND, period - 1)[:, :, :BAND_WINDOW]
    q_chunk = np.arange(TQ_BAND)[:, None] // CHUNK
    k_chunk = np.arange(BAND_WINDOW)[None, :] // CHUNK
    in_band = (k_chunk >= q_chunk) & (k_chunk <= q_chunk + LEFT_CHUNKS)
    return jnp.where(jnp.asarray(in_band)[None], bias * LOG2E, -jnp.inf)


def score_bound(gq, gk, bias_max=0.0):
    qk = HEAD_DIM * ATTN_SCALE * jnp.max(jnp.abs(gq)) * jnp.max(jnp.abs(gk))
    return ((qk + bias_max) * (LOG2E * BOUND_SLACK)).astype(F32)


def band_attention(qn, kn, proj, bias, score_bound, batch, seq):
    t, width = qn.shape
    n_pair = width // LANES
    seq_block = lambda col0: pl.BlockSpec((seq, LANES), lambda b, p: (b, col0 + p))
    return pl.pallas_call(
        functools.partial(_band_kernel, seq=seq),
        grid=(batch, n_pair),
        in_specs=[pl.BlockSpec(memory_space=pltpu.SMEM),
                  seq_block(0), seq_block(0), seq_block(2 * n_pair),
                  pl.BlockSpec((LANES // HEAD_DIM, TQ_BAND, BAND_WINDOW), lambda b, p: (p, 0, 0))],
        out_specs=seq_block(0),
        out_shape=jax.ShapeDtypeStruct((t, width), BF16),
        compiler_params=_params("parallel", "parallel"),
        name="band_attention",
    )(score_bound.reshape(1, 1), qn, kn, proj, bias)


def _out_proj_kernel(*refs):
    x_ref, *part_refs, w_ref, o_ref = refs
    acc = x_ref[...]
    row = 0
    for part in part_refs:
        d = part.shape[1]
        acc = acc + jnp.dot(part[...], w_ref[row:row + d, :], preferred_element_type=F32)
        row += d
    o_ref[...] = acc


def out_proj_residual(x, parts, w):
    t, d = x.shape
    tm = min(TM_PROJ, t)
    return pl.pallas_call(
        _out_proj_kernel,
        grid=(t // tm,),
        in_specs=([pl.BlockSpec((tm, d), lambda i: (i, 0))]
                  + [pl.BlockSpec((tm, p.shape[1]), lambda i: (i, 0)) for p in parts]
                  + [pl.BlockSpec(w.shape, lambda i: (0, 0))]),
        out_specs=pl.BlockSpec((tm, d), lambda i: (i, 0)),
        out_shape=jax.ShapeDtypeStruct((t, d), F32),
        compiler_params=_params("parallel"),
        name="out_proj_residual",
    )(x, *parts, w)


def _swiglu_kernel(x_ref, g_ref, wg_ref, wu_ref, wd_ref, o_ref, h_ref, acc_ref):
    j = pl.program_id(1)

    @pl.when(j == 0)
    def _():
        x = x_ref[...]
        h_ref[...] = _rms(x, g_ref[...]).astype(BF16)
        acc_ref[...] = x

    h = h_ref[...]
    gate = jnp.dot(h, wg_ref[...], preferred_element_type=F32)
    up = jnp.dot(h, wu_ref[...], preferred_element_type=F32)
    act = (gate * jax.nn.sigmoid(gate) * up).astype(BF16)
    acc_ref[...] += jnp.dot(act, wd_ref[...], preferred_element_type=F32)

    @pl.when(j == pl.num_programs(1) - 1)
    def _():
        o_ref[...] = acc_ref[...]


def swiglu_residual(x, g, wg, wu, wd):
    t, d = x.shape
    f = wg.shape[1]
    tm = min(TM_PROJ, t)
    tf = TF_DENSE if f % TF_DENSE == 0 else f
    return pl.pallas_call(
        _swiglu_kernel,
        grid=(t // tm, f // tf),
        in_specs=[pl.BlockSpec((tm, d), lambda i, j: (i, 0)),
                  pl.BlockSpec((1, d), lambda i, j: (0, 0)),
                  pl.BlockSpec((d, tf), lambda i, j: (0, j)),
                  pl.BlockSpec((d, tf), lambda i, j: (0, j)),
                  pl.BlockSpec((tf, d), lambda i, j: (j, 0))],
        out_specs=pl.BlockSpec((tm, d), lambda i, j: (i, 0)),
        out_shape=jax.ShapeDtypeStruct((t, d), F32),
        scratch_shapes=[pltpu.VMEM((tm, d), BF16), pltpu.VMEM((tm, d), F32)],
        compiler_params=_params("parallel", "arbitrary"),
        name="swiglu_residual",
    )(x, g.reshape(1, d), wg, wu, wd)


def _router_kernel(x_ref, g_ref, wr_ref, h_ref, pick_ref, gate_ref, count_ref, carry_ref, *, tm):
    i = pl.program_id(0)

    @pl.when(i == 0)
    def _():
        carry_ref[...] = jnp.zeros_like(carry_ref)

    h = _rms(x_ref[...], g_ref[...])
    h_ref[...] = h
    logits = lax.dot_general(wr_ref[...], h, _NT, preferred_element_type=F32,
                             precision=lax.Precision.HIGHEST)
    e_idx = lax.broadcasted_iota(jnp.int32, (N_EXPERTS, tm), 0)

    def top1(vals):
        best = jnp.max(vals, axis=0, keepdims=True)
        arg = jnp.min(jnp.where(vals == best, e_idx, N_EXPERTS), axis=0, keepdims=True)
        return best, arg

    v1, i1 = top1(logits)
    v2, i2 = top1(jnp.where(e_idx == i1, -jnp.inf, logits))
    ratio = jnp.exp(v2 - v1)
    g1 = 1.0 / (1.0 + ratio)
    g2 = ratio / (1.0 + ratio)

    picked = jnp.where((e_idx == i1) | (e_idx == i2), 1.0, 0.0)
    a = lax.broadcasted_iota(jnp.int32, (tm, tm), 0)
    b = lax.broadcasted_iota(jnp.int32, (tm, tm), 1)
    before = jnp.where(a < b, 1.0, 0.0).astype(BF16)
    rank = carry_ref[...] + jnp.dot(picked.astype(BF16), before, preferred_element_type=F32)
    rank = rank.astype(jnp.int32)
    r1 = jnp.sum(jnp.where(e_idx == i1, rank, 0), axis=0, keepdims=True)
    r2 = jnp.sum(jnp.where(e_idx == i2, rank, 0), axis=0, keepdims=True)
    carry_ref[...] += jnp.sum(picked, axis=1, keepdims=True)

    pick_ref[...] = jnp.zeros_like(pick_ref)
    gate_ref[...] = jnp.zeros_like(gate_ref)
    for r, val in enumerate((i1, i2, r1, r2)):
        pick_ref[r:r + 1, :] = val
    for r, val in enumerate((g1, g2)):
        gate_ref[r:r + 1, :] = val
    count_ref[...] = jnp.broadcast_to(carry_ref[...], count_ref.shape).astype(jnp.int32)


def router(x, g, w_router):
    t, d = x.shape
    tm = min(TM_ROUTE, t)
    return pl.pallas_call(
        functools.partial(_router_kernel, tm=tm),
        grid=(t // tm,),
        in_specs=[pl.BlockSpec((tm, d), lambda i: (i, 0)),
                  pl.BlockSpec((1, d), lambda i: (0, 0)),
                  pl.BlockSpec((N_EXPERTS, d), lambda i: (0, 0))],
        out_specs=[pl.BlockSpec((tm, d), lambda i: (i, 0)),
                   pl.BlockSpec((8, tm), lambda i: (0, i)),
                   pl.BlockSpec((8, tm), lambda i: (0, i)),
                   pl.BlockSpec((N_EXPERTS, LANES), lambda i: (0, 0))],
        out_shape=[jax.ShapeDtypeStruct((t, d), F32),
                   jax.ShapeDtypeStruct((8, t), jnp.int32),
                   jax.ShapeDtypeStruct((8, t), F32),
                   jax.ShapeDtypeStruct((N_EXPERTS, LANES), jnp.int32)],
        scratch_shapes=[pltpu.VMEM((N_EXPERTS, 1), F32)],
        compiler_params=_params("arbitrary"),
        name="router",
    )(x, g.reshape(1, d), w_router.T)


def _row_copy(src_ref, src_row, dst_ref, dst_row, sem):
    return pltpu.make_async_copy(src_ref.at[pl.ds(src_row, 1)], dst_ref.at[pl.ds(dst_row, 1)], sem)


def _dispatch_kernel(dest_ref, h_ref, init_ref, xs_ref, sem, *, tm):
    del init_ref

    def issue(t, c):
        for k in range(TOP_K):
            _row_copy(h_ref, t, xs_ref, dest_ref[0, k, t], sem).start()
        return c

    lax.fori_loop(0, tm, issue, 0, unroll=ROW_DMA_UNROLL)
    for k in range(TOP_K):
        pltpu.make_async_copy(h_ref, xs_ref.at[pl.ds(0, tm)], sem).wait()


def dispatch_rows(h, dest, n_rows):
    t, d = h.shape
    tm = min(TM_ROWS, t)
    return pl.pallas_call(
        functools.partial(_dispatch_kernel, tm=tm),
        grid=(t // tm,),
        in_specs=[pl.BlockSpec((1, TOP_K, tm), lambda i: (i, 0, 0), memory_space=pltpu.SMEM),
                  pl.BlockSpec((tm, d), lambda i: (i, 0)),
                  pl.BlockSpec(memory_space=pl.ANY)],
        out_specs=pl.BlockSpec(memory_space=pl.ANY),
        out_shape=jax.ShapeDtypeStruct((n_rows, d), h.dtype),
        scratch_shapes=[pltpu.SemaphoreType.DMA(())],
        input_output_aliases={2: 0},
        compiler_params=_params("arbitrary"),
        name="dispatch_rows",
    )(dest, h, jnp.zeros((n_rows, d), h.dtype))


def _combine_kernel(dest_ref, x_ref, gate_ref, ys_ref, o_ref, buf_ref, sem, *, tm):
    def issue(t, c):
        for k in range(TOP_K):
            _row_copy(ys_ref, dest_ref[0, k, t], buf_ref.at[k], t, sem).start()
        return c

    lax.fori_loop(0, tm, issue, 0, unroll=ROW_DMA_UNROLL)
    for k in range(TOP_K):
        pltpu.make_async_copy(ys_ref.at[pl.ds(0, tm)], buf_ref.at[k], sem).wait()
    gate = gate_ref[...]
    o_ref[...] = x_ref[...] + gate[:, 0:1] * buf_ref[0] + gate[:, 1:2] * buf_ref[1]


def combine_rows(x, gates, ys, dest):
    t, d = x.shape
    tm = min(TM_ROWS, t)
    return pl.pallas_call(
        functools.partial(_combine_kernel, tm=tm),
        grid=(t // tm,),
        in_specs=[pl.BlockSpec((1, TOP_K, tm), lambda i: (i, 0, 0), memory_space=pltpu.SMEM),
                  pl.BlockSpec((tm, d), lambda i: (i, 0)),
                  pl.BlockSpec((tm, TOP_K), lambda i: (i, 0)),
                  pl.BlockSpec(memory_space=pl.ANY)],
        out_specs=pl.BlockSpec((tm, d), lambda i: (i, 0)),
        out_shape=jax.ShapeDtypeStruct((t, d), F32),
        scratch_shapes=[pltpu.VMEM((TOP_K, tm, d), F32), pltpu.SemaphoreType.DMA(())],
        compiler_params=_params("arbitrary"),
        name="combine_rows",
    )(dest, x, gates, ys)


def _expert_ffn_kernel(expert_ref, live_ref, xs_ref, wg_ref, wu_ref, wd_ref, o_ref, h_ref, acc_ref):
    del expert_ref
    i, j = pl.program_id(0), pl.program_id(1)
    live = live_ref[i] > 0

    @pl.when(j == 0)
    def _():
        h_ref[...] = xs_ref[...].astype(BF16)
        acc_ref[...] = jnp.zeros_like(acc_ref)

    @pl.when(live)
    def _():
        h = h_ref[...]
        gate = jnp.dot(h, wg_ref[0], preferred_element_type=F32)
        up = jnp.dot(h, wu_ref[0], preferred_element_type=F32)
        act = (gate * jax.nn.sigmoid(gate) * up).astype(BF16)
        acc_ref[...] += jnp.dot(act, wd_ref[0], preferred_element_type=F32)

    @pl.when(j == pl.num_programs(1) - 1)
    def _():
        o_ref[...] = acc_ref[...]


def expert_ffn(xs, tile_expert, tile_live, wg, wu, wd):
    r, d = xs.shape
    f = wg.shape[2]
    tm = TM_EXPERT
    tf = TF_EXPERT if f % TF_EXPERT == 0 else f
    grid_spec = pltpu.PrefetchScalarGridSpec(
        num_scalar_prefetch=2,
        grid=(r // tm, f // tf),
        in_specs=[pl.BlockSpec((tm, d), lambda i, j, e, v: (i, 0)),
                  pl.BlockSpec((1, d, tf), lambda i, j, e, v: (e[i], 0, j)),
                  pl.BlockSpec((1, d, tf), lambda i, j, e, v: (e[i], 0, j)),
                  pl.BlockSpec((1, tf, d), lambda i, j, e, v: (e[i], j, 0))],
        out_specs=pl.BlockSpec((tm, d), lambda i, j, e, v: (i, 0)),
        scratch_shapes=[pltpu.VMEM((tm, d), BF16), pltpu.VMEM((tm, d), F32)])
    return pl.pallas_call(
        _expert_ffn_kernel,
        grid_spec=grid_spec,
        out_shape=jax.ShapeDtypeStruct((r, d), F32),
        compiler_params=_params("parallel", "arbitrary"),
        name="expert_ffn",
    )(tile_expert, tile_live, xs, wg, wu, wd)


def moe_residual(x, g, w_router, wg, wu, wd):
    t, d = x.shape
    tm = TM_EXPERT
    n_rows = TOP_K * t + N_EXPERTS * tm
    n_tiles = n_rows // tm
    h, picks, gates, counts = router(x, g, w_router)

    counts = counts[:, 0]
    padded = (counts + tm - 1) // tm * tm
    ends = jnp.cumsum(padded)
    starts = ends - padded
    experts, ranks = picks[0:TOP_K], picks[TOP_K:2 * TOP_K]
    dest = jnp.sum(jnp.where(experts[None] == jnp.arange(N_EXPERTS)[:, None, None],
                             starts[:, None, None], 0), axis=0) + ranks
    tile_start = jnp.arange(n_tiles, dtype=jnp.int32) * tm
    tile_expert = jnp.minimum(jnp.sum(tile_start[:, None] >= ends[None, :], axis=1),
                              N_EXPERTS - 1).astype(jnp.int32)
    tile_live = (tile_start < ends[-1]).astype(jnp.int32)

    tr = min(TM_ROWS, t)
    dest_tiles = dest.reshape(TOP_K, t // tr, tr).transpose(1, 0, 2)
    xs = dispatch_rows(h, dest_tiles, n_rows)
    ys = expert_ffn(xs, tile_expert, tile_live, wg, wu, wd)
    return combine_rows(x, gates[0:TOP_K].T, ys, dest_tiles)


def kernel(x, positions, ev_attn_norm, ev_w_in, ev_q_norm, ev_k_norm, ev_lambda_q1, ev_lambda_k1, ev_lambda_q2, ev_lambda_k2, ev_subln, ev_w_out, ev_ffn_norm, ev_w_gate, ev_w_up, ev_w_down, od_attn_norm, od_w_qkv, od_q_norm, od_k_norm, od_rel_bias, od_w_out, od_ffn_norm, od_router, od_we_gate, od_we_up, od_we_down):
    batch, seq, d_model = x.shape
    depth = ev_attn_norm.shape[0] + od_attn_norm.shape[0]
    bf = lambda w: w.astype(BF16)
    x = x.reshape(batch * seq, d_model)
    d_sb = N_HEADS_SB * HEAD_DIM
    d_diff = N_HEADS_DIFF * 2 * HEAD_DIM
    for layer in range(depth):
        i = layer // 2
        if layer % 2 == 0:
            col_scale = jnp.where(jnp.arange(ev_w_in.shape[2]) < d_sb, ATTN_SCALE * LOG2E, 1.0)
            proj = norm_proj(x, ev_attn_norm[i], bf(ev_w_in[i] * col_scale))
            o_sb = sb_attention(proj, batch, seq)
            qd, kd = qk_prep(proj, 3 * d_sb // d_diff, 3 * d_sb // d_diff + 1, d_diff,
                             ev_q_norm[i], ev_k_norm[i], positions)
            lam_init = 0.8 - 0.6 * math.exp(-0.3 * layer)
            lam_vecs = jnp.stack([ev_lambda_q1[i], ev_lambda_k1[i], ev_lambda_q2[i], ev_lambda_k2[i]])
            o_df = diff_attention(qd, kd, proj, score_bound(ev_q_norm[i], ev_k_norm[i]),
                                  lam_vecs.astype(F32), ev_subln[i], lam_init, batch, seq)
            x = out_proj_residual(x, [o_sb, o_df], bf(ev_w_out[i]))
            x = swiglu_residual(x, ev_ffn_norm[i], bf(ev_w_gate[i]), bf(ev_w_up[i]), bf(ev_w_down[i]))
        else:
            proj = norm_proj(x, od_attn_norm[i], bf(od_w_qkv[i]))
            qn, kn = qk_prep(proj, 0, 1, d_model, od_q_norm[i], od_k_norm[i])
            bound = score_bound(od_q_norm[i], od_k_norm[i], jnp.max(jnp.abs(od_rel_bias[i])))
            o_band = band_attention(qn, kn, proj, band_bias_table(od_rel_bias[i]), bound, batch, seq)
            x = out_proj_residual(x, [o_band], bf(od_w_out[i]))
            x = moe_residual(x, od_ffn_norm[i], od_router[i], bf(od_we_gate[i]), bf(od_we_up[i]),
                             bf(od_we_down[i]))
    return x.reshape(batch, seq, d_model)
```
